```python
import jax, jax.numpy as jnp
from jax import lax
import numpy as np

D_MODEL = 1024
BATCH = 1
SEQ = 16384
DEPTH = 2
DEC_BATCH = 32
DEC_SEQ = 1
PAST_LEN = 16384
PAGE_SIZE = 128

SSD_WIDTH = D_MODEL // 2
SSD_HEAD_DIM = 64
SSD_HEADS = SSD_WIDTH // SSD_HEAD_DIM
SSD_GROUPS = 2
SSD_STATE = 128
SSD_CONV = 4
SSD_CHUNK = 128
SSD_CONV_DIM = SSD_WIDTH + 2 * SSD_GROUPS * SSD_STATE
SB_HEADS = 8
SB_HEAD_DIM = 64
SB_WIDTH = SB_HEADS * SB_HEAD_DIM
SB_BLOCK = 128
SB_BIAS_MIN = 3.0
SB_BIAS_MAX = 10.0
RET_HEADS = 4
RET_DK = 64
RET_DV = 128
RET_QK_WIDTH = RET_HEADS * RET_DK
RET_V_WIDTH = RET_HEADS * RET_DV
RET_CHUNK = 128
ROPE_BASE = 10000.0
D_FF = 4 * D_MODEL
N_BRANCH = 3
EPS = 1e-6
IN_DIM = (SSD_WIDTH + SSD_CONV_DIM + SSD_HEADS) + 3 * SB_WIDTH + (2 * RET_QK_WIDTH + 2 * RET_V_WIDTH)

kernel_name = 'hybrid_ssd_stickbreak_retention_decoder_step'

F32 = jnp.float32


def _split_points():
    sizes = [SSD_WIDTH, SSD_CONV_DIM, SSD_HEADS, SB_WIDTH, SB_WIDTH, SB_WIDTH,
             RET_QK_WIDTH, RET_QK_WIDTH, RET_V_WIDTH, RET_V_WIDTH]
    return [int(i) for i in np.cumsum(sizes)[:-1]]


def rms_norm(x, w):
    xf = x.astype(F32)
    y = xf * lax.rsqrt(jnp.mean(xf * xf, axis=-1, keepdims=True) + EPS)
    return (y * w.astype(F32)).astype(x.dtype)


def rotary(x, pos):
    half = x.shape[-1] // 2
    inv = ROPE_BASE ** (-jnp.arange(half, dtype=F32) / half)
    ang = pos.astype(F32)[:, None] * inv[None, :]
    cos = jnp.cos(ang)[None, :, None, :]
    sin = jnp.sin(ang)[None, :, None, :]
    xf = x.astype(F32)
    x1, x2 = xf[..., :half], xf[..., half:]
    return jnp.concatenate([x1 * cos - x2 * sin, x1 * sin + x2 * cos], axis=-1)


def causal_dwconv(x_ext, w, b):
    y = lax.conv_general_dilated(x_ext, w[:, None, :].astype(x_ext.dtype), (1,), 'VALID',
                                 dimension_numbers=('NWC', 'WIO', 'NWC'),
                                 feature_group_count=x_ext.shape[-1])
    return y + b.astype(x_ext.dtype)


def ssd_chunked(xh, dt, a_neg, bm, cm):
    b, L, h, p = xh.shape
    g, n = bm.shape[2], bm.shape[3]
    r = h // g
    lc = SSD_CHUNK
    c = L // lc
    xdt = (xh.astype(F32) * dt[..., None]).reshape(b, c, lc, g, r, p)
    la = jnp.moveaxis((dt * a_neg).reshape(b, c, lc, g, r), 2, -1)
    la_cum = jnp.cumsum(la, axis=-1)
    bc = bm.astype(F32).reshape(b, c, lc, g, n)
    cc = cm.astype(F32).reshape(b, c, lc, g, n)
    causal = jnp.tril(jnp.ones((lc, lc), dtype=bool))
    seg = la_cum[..., :, None] - la_cum[..., None, :]
    decay = jnp.exp(jnp.where(causal, seg, -jnp.inf))
    cb = jnp.einsum('bclgn,bcsgn->bcgls', cc, bc)
    y_diag = jnp.einsum('bcgls,bcgrls,bcsgrp->bclgrp', cb, decay, xdt)
    to_end = jnp.exp(la_cum[..., -1:] - la_cum)
    chunk_states = jnp.einsum('bclgn,bcgrl,bclgrp->bcgrpn', bc, to_end, xdt)
    chunk_decay = jnp.exp(la_cum[..., -1])

    def carry(s, inp):
        st, dec = inp
        return s * dec[..., None, None] + st, s

    init = jnp.zeros((b, g, r, p, n), F32)
    final, entering = lax.scan(carry, init, (jnp.moveaxis(chunk_states, 1, 0),
                                             jnp.moveaxis(chunk_decay, 1, 0)))
    entering = jnp.moveaxis(entering, 0, 1)
    y_off = jnp.einsum('bclgn,bcgrpn,bcgrl->bclgrp', cc, entering, jnp.exp(la_cum))
    y = (y_diag + y_off).reshape(b, L, h, p)
    return y, final.reshape(b, h, p, n)


def ssd_recurrent(xh, dt, a_neg, bm, cm, state0):
    b, T, h, p = xh.shape
    g, n = bm.shape[2], bm.shape[3]
    r = h // g

    def step(s, inp):
        x_t, dt_t, b_t, c_t = inp
        da = jnp.exp(dt_t * a_neg).reshape(b, g, r)
        xdt = (x_t * dt_t[..., None]).reshape(b, g, r, p)
        s = s * da[..., None, None] + jnp.einsum('bgrp,bgn->bgrpn', xdt, b_t)
        y = jnp.einsum('bgrpn,bgn->bgrp', s, c_t).reshape(b, h, p)
        return s, y

    s0 = state0.astype(F32).reshape(b, g, r, p, n)
    xs = (jnp.moveaxis(xh.astype(F32), 1, 0), jnp.moveaxis(dt, 1, 0),
          jnp.moveaxis(bm.astype(F32), 1, 0), jnp.moveaxis(cm.astype(F32), 1, 0))
    final, ys = lax.scan(step, s0, xs)
    return jnp.moveaxis(ys, 0, 1), final.reshape(b, h, p, n)


def ssd_branch(z, xbc, dt_raw, conv_prev, ssm_prev, conv_w, conv_b, dt_bias, a_log, d_skip, norm_w):
    x_ext = jnp.concatenate([conv_prev.astype(xbc.dtype), xbc], axis=1)
    conv_new = x_ext[:, -(SSD_CONV - 1):]
    xbc_c = jax.nn.silu(causal_dwconv(x_ext, conv_w, conv_b))
    xs, bm, cm = jnp.split(xbc_c, [SSD_WIDTH, SSD_WIDTH + SSD_GROUPS * SSD_STATE], axis=-1)
    b, L, _ = xs.shape
    xh = xs.reshape(b, L, SSD_HEADS, SSD_HEAD_DIM)
    bm = bm.reshape(b, L, SSD_GROUPS, SSD_STATE)
    cm = cm.reshape(b, L, SSD_GROUPS, SSD_STATE)
    dt = jax.nn.softplus(dt_raw.astype(F32) + dt_bias.astype(F32))
    a_neg = -jnp.exp(a_log.astype(F32))
    if ssm_prev is None:
        y, s_new = ssd_chunked(xh, dt, a_neg, bm, cm)
    else:
        y, s_new = ssd_recurrent(xh, dt, a_neg, bm, cm, ssm_prev)
    y = y + d_skip.astype(F32)[:, None] * xh.astype(F32)
    u = (y.reshape(b, L, SSD_WIDTH) * jax.nn.silu(z.astype(F32))).reshape(b, L, SSD_GROUPS, -1)
    u = u * lax.rsqrt(jnp.mean(u * u, axis=-1, keepdims=True) + EPS)
    out = u.reshape(b, L, SSD_WIDTH) * norm_w.astype(F32)
    return out.astype(z.dtype), conv_new, s_new


def sb_weights(z, causal):
    log_1m = jnp.where(causal, jax.nn.log_sigmoid(-z), 0.0)
    suffix = lax.cumsum(log_1m, axis=z.ndim - 1, reverse=True) - log_1m
    return jnp.where(causal, jnp.exp(jax.nn.log_sigmoid(z) + suffix), 0.0)


def sb_prompt(q, k, v, bias):
    b, L, h, d = q.shape
    nb = L // SB_BLOCK
    qb = jnp.moveaxis(q.reshape(b, nb, SB_BLOCK, h, d), 1, 0)
    starts = jnp.arange(nb, dtype=jnp.int32) * SB_BLOCK
    key_pos = jnp.arange(L, dtype=jnp.int32)
    scale = d ** -0.5
    bias_f = bias.astype(F32)[:, None, None]

    def one_block(args):
        q_blk, start = args
        z = jnp.einsum('bqhd,bkhd->bhqk', q_blk, k).astype(F32) * scale + bias_f
        q_pos = start + jnp.arange(SB_BLOCK, dtype=jnp.int32)
        causal = key_pos[None, :] < q_pos[:, None]
        a = sb_weights(z, causal)
        return jnp.einsum('bhqk,bkhd->bqhd', a.astype(v.dtype), v)

    out = lax.map(one_block, (qb, starts))
    return jnp.moveaxis(out, 0, 1).reshape(b, L, h, d)


def sb_decode(q, k_new, v_new, k_past, v_past, bias):
    b, T, h, d = q.shape
    P = k_past.shape[1]
    scale = d ** -0.5
    z = jnp.concatenate([jnp.einsum('bthd,bkhd->bhtk', q, k_past),
                         jnp.einsum('bthd,bkhd->bhtk', q, k_new)], axis=-1).astype(F32) * scale
    z = z + bias.astype(F32)[:, None, None]
    key_pos = jnp.arange(P + T, dtype=jnp.int32)
    q_pos = P + jnp.arange(T, dtype=jnp.int32)
    causal = key_pos[None, :] < q_pos[:, None]
    a = sb_weights(z, causal).astype(v_new.dtype)
    return (jnp.einsum('bhtk,bkhd->bthd', a[..., :P], v_past)
            + jnp.einsum('bhtk,bkhd->bthd', a[..., P:], v_new))


def retention_chunked(q, k, v, log_gamma):
    b, L, h, dk = q.shape
    dv = v.shape[-1]
    lc = RET_CHUNK
    c = L // lc
    qc = q.reshape(b, c, lc, h, dk)
    kc = k.reshape(b, c, lc, h, dk)
    vc = v.reshape(b, c, lc, h, dv)
    idx = jnp.arange(lc, dtype=F32)
    rel = idx[:, None] - idx[None, :]
    causal = rel >= 0
    dmat = jnp.exp(jnp.where(causal[None], rel[None] * log_gamma[:, None, None], -jnp.inf))
    scores = jnp.einsum('bclhd,bcshd->bchls', qc, kc) * dmat
    y_inner = jnp.einsum('bchls,bcshe->bclhe', scores, vc)
    to_end = jnp.exp((lc - 1 - idx)[None, :] * log_gamma[:, None])
    chunk_states = jnp.einsum('bclhd,hl,bclhe->bchde', kc, to_end, vc)
    chunk_decay = jnp.exp(lc * log_gamma)

    def carry(s, st):
        return s * chunk_decay[:, None, None] + st, s

    final, entering = lax.scan(carry, jnp.zeros((b, h, dk, dv), F32), jnp.moveaxis(chunk_states, 1, 0))
    entering = jnp.moveaxis(entering, 0, 1)
    from_start = jnp.exp((idx + 1.0)[None, :] * log_gamma[:, None])
    y_cross = jnp.einsum('bclhd,bchde,hl->bclhe', qc, entering, from_start)
    return (y_inner + y_cross).reshape(b, L, h, dv), final


def retention_recurrent(q, k, v, log_gamma, state0):
    gamma = jnp.exp(log_gamma)

    def step(s, inp):
        q_t, k_t, v_t = inp
        s = s * gamma[:, None, None] + jnp.einsum('bhd,bhe->bhde', k_t, v_t)
        return s, jnp.einsum('bhd,bhde->bhe', q_t, s)

    final, ys = lax.scan(step, state0, (jnp.moveaxis(q, 1, 0), jnp.moveaxis(k, 1, 0), jnp.moveaxis(v, 1, 0)))
    return jnp.moveaxis(ys, 0, 1), final


def retention_branch(rq, rk, rv, rg, pos, ret_prev, norm_w):
    b, L, _ = rq.shape
    log_gamma = jnp.log1p(-jnp.exp2(-5.0 - jnp.arange(RET_HEADS, dtype=F32)))
    q = rotary(rq.reshape(b, L, RET_HEADS, RET_DK), pos)
    k = rotary(rk.reshape(b, L, RET_HEADS, RET_DK), pos) * (RET_DK ** -0.5)
    v = rv.reshape(b, L, RET_HEADS, RET_DV).astype(F32)
    if ret_prev is None:
        y, s_new = retention_chunked(q, k, v, log_gamma)
    else:
        y, s_new = retention_recurrent(q, k, v, log_gamma, ret_prev.astype(F32))
    mu = jnp.mean(y, axis=-1, keepdims=True)
    yc = y - mu
    y = yc * lax.rsqrt(jnp.mean(yc * yc, axis=-1, keepdims=True) + EPS)
    y = y.reshape(b, L, RET_V_WIDTH) * norm_w.astype(F32)
    y = jax.nn.silu(rg.astype(F32)) * y
    return y.astype(rq.dtype), s_new


def trunk_layer(x, pos, lw, conv_prev, ssm_prev, ret_prev, k_past, v_past):
    b, L, _ = x.shape
    h = rms_norm(x, lw['norm_mix'])
    proj = h @ lw['w_in']
    z, xbc, dt_raw, sq, sk, sv, rq, rk, rv, rg = jnp.split(proj, _split_points(), axis=-1)
    o_ssd, conv_new, ssm_new = ssd_branch(z, xbc, dt_raw, conv_prev, ssm_prev, lw['conv_w'], lw['conv_b'],
                                          lw['dt_bias'], lw['a_log'], lw['d_skip'], lw['ssd_norm_w'])
    q = rms_norm(sq.reshape(b, L, SB_HEADS, SB_HEAD_DIM), lw['sb_q_norm'])
    k = rms_norm(sk.reshape(b, L, SB_HEADS, SB_HEAD_DIM), lw['sb_k_norm'])
    v = sv.reshape(b, L, SB_HEADS, SB_HEAD_DIM)
    if k_past is None:
        o_sb = sb_prompt(q, k, v, lw['sb_bias'])
    else:
        o_sb = sb_decode(q, k, v, k_past, v_past, lw['sb_bias'])
    o_sb = o_sb.reshape(b, L, SB_WIDTH).astype(x.dtype)
    o_ret, ret_new = retention_branch(rq, rk, rv, rg, pos, ret_prev, lw['ret_norm_w'])
    gates = jax.nn.sigmoid((h @ lw['w_gate'] + lw['b_gate']).astype(F32))
    gates = gates.reshape(b, L, N_BRANCH, D_MODEL).astype(x.dtype)
    merged = (gates[:, :, 0] * (o_ssd @ lw['w_br_ssd'])
              + gates[:, :, 1] * (o_sb @ lw['w_br_sb'])
              + gates[:, :, 2] * (o_ret @ lw['w_br_ret']))
    x = x + merged @ lw['w_out']
    h2 = rms_norm(x, lw['norm_mlp'])
    x = x + jnp.square(jax.nn.relu(h2 @ lw['w_up'])) @ lw['w_down']
    dt_ = x.dtype
    return x, (k.astype(dt_), v.astype(dt_), conv_new.astype(dt_), ssm_new.astype(dt_), ret_new.astype(dt_))


def setup_inputs(seed: int = 0) -> dict:
    key = jax.random.key(seed)
    ks = jax.random.split(key, 32)
    n_pages = PAST_LEN // PAGE_SIZE
    n_used = DEC_BATCH * n_pages
    n_phys = n_used + n_used // 4

    def nrm(k, shape, scale):
        return jax.random.normal(k, shape, F32) * scale

    def gain(k, shape):
        return 1.0 + 0.02 * jax.random.normal(k, shape, F32)

    dt0 = jnp.exp(jax.random.uniform(ks[20], (DEPTH, SSD_HEADS), F32, np.log(1e-3), np.log(1e-1)))
    sb_bias0 = -jnp.linspace(SB_BIAS_MIN, SB_BIAS_MAX, SB_HEADS, dtype=F32)
    return {
        'x_prompt': nrm(ks[0], (BATCH, SEQ, D_MODEL), 1.0),
        'x_sample': nrm(ks[1], (DEC_BATCH, DEC_SEQ, D_MODEL), 1.0),
        'cache_sb_k': nrm(ks[2], (DEPTH, n_phys, PAGE_SIZE, SB_HEADS, SB_HEAD_DIM), 1.0),
        'cache_sb_v': nrm(ks[3], (DEPTH, n_phys, PAGE_SIZE, SB_HEADS, SB_HEAD_DIM), 1.0),
        'page_table': jax.random.permutation(ks[4], n_phys)[:n_used].reshape(DEC_BATCH, n_pages).astype(jnp.int32),
        'state_ssd_conv': nrm(ks[5], (DEPTH, DEC_BATCH, SSD_CONV - 1, SSD_CONV_DIM), 1.0),
        'state_ssd': nrm(ks[6], (DEPTH, DEC_BATCH, SSD_HEADS, SSD_HEAD_DIM, SSD_STATE), 0.5),
        'state_ret': nrm(ks[7], (DEPTH, DEC_BATCH, RET_HEADS, RET_DK, RET_DV), 0.5),
        'norm_mix': gain(ks[8], (DEPTH, D_MODEL)),
        'w_in': nrm(ks[9], (DEPTH, D_MODEL, IN_DIM), D_MODEL ** -0.5),
        'conv_w': nrm(ks[10], (DEPTH, SSD_CONV, SSD_CONV_DIM), SSD_CONV ** -0.5),
        'conv_b': nrm(ks[11], (DEPTH, SSD_CONV_DIM), 0.02),
        'dt_bias': dt0 + jnp.log(-jnp.expm1(-dt0)),
        'a_log': jnp.log(jax.random.uniform(ks[12], (DEPTH, SSD_HEADS), F32, 1.0, 16.0)),
        'd_skip': gain(ks[13], (DEPTH, SSD_HEADS)),
        'ssd_norm_w': gain(ks[14], (DEPTH, SSD_WIDTH)),
        'sb_q_norm': gain(ks[15], (DEPTH, SB_HEAD_DIM)),
        'sb_k_norm': gain(ks[16], (DEPTH, SB_HEAD_DIM)),
        'sb_bias': sb_bias0[None, :] + nrm(ks[28], (DEPTH, SB_HEADS), 0.01),
        'ret_norm_w': gain(ks[17], (DEPTH, RET_V_WIDTH)),
        'w_gate': nrm(ks[18], (DEPTH, D_MODEL, N_BRANCH * D_MODEL), D_MODEL ** -0.5),
        'b_gate': nrm(ks[19], (DEPTH, N_BRANCH * D_MODEL), 0.02),
        'w_br_ssd': nrm(ks[21], (DEPTH, SSD_WIDTH, D_MODEL), SSD_WIDTH ** -0.5),
        'w_br_sb': nrm(ks[22], (DEPTH, SB_WIDTH, D_MODEL), SB_WIDTH ** -0.5),
        'w_br_ret': nrm(ks[23], (DEPTH, RET_V_WIDTH, D_MODEL), RET_V_WIDTH ** -0.5),
        'w_out': nrm(ks[24], (DEPTH, D_MODEL, D_MODEL), D_MODEL ** -0.5),
        'norm_mlp': gain(ks[25], (DEPTH, D_MODEL)),
        'w_up': nrm(ks[26], (DEPTH, D_MODEL, D_FF), D_MODEL ** -0.5),
        'w_down': nrm(ks[27], (DEPTH, D_FF, D_MODEL), D_FF ** -0.5),
    }


def reference(x_prompt, x_sample, cache_sb_k, cache_sb_v, page_table, state_ssd_conv, state_ssd, state_ret,
              norm_mix, w_in, conv_w, conv_b, dt_bias, a_log, d_skip, ssd_norm_w, sb_q_norm, sb_k_norm,
              sb_bias, ret_norm_w, w_gate, b_gate, w_br_ssd, w_br_sb, w_br_ret, w_out, norm_mlp, w_up, w_down):
    b_p, seq, _ = x_prompt.shape
    b_s, t_new, _ = x_sample.shape
    past = page_table.shape[1] * PAGE_SIZE
    pos_p = jnp.arange(seq, dtype=jnp.int32)
    pos_s = past + jnp.arange(t_new, dtype=jnp.int32)
    conv_zero = jnp.zeros((b_p, SSD_CONV - 1, SSD_CONV_DIM), x_prompt.dtype)
    yp, ys = x_prompt, x_sample
    kp, vp, ks_, vs_, cp, cs, sp, ss, rp, rs = [], [], [], [], [], [], [], [], [], []
    for l in range(DEPTH):
        lw = {'norm_mix': norm_mix[l], 'w_in': w_in[l], 'conv_w': conv_w[l], 'conv_b': conv_b[l],
              'dt_bias': dt_bias[l], 'a_log': a_log[l], 'd_skip': d_skip[l], 'ssd_norm_w': ssd_norm_w[l],
              'sb_q_norm': sb_q_norm[l], 'sb_k_norm': sb_k_norm[l], 'sb_bias': sb_bias[l],
              'ret_norm_w': ret_norm_w[l],
              'w_gate': w_gate[l], 'b_gate': b_gate[l], 'w_br_ssd': w_br_ssd[l], 'w_br_sb': w_br_sb[l],
              'w_br_ret': w_br_ret[l], 'w_out': w_out[l], 'norm_mlp': norm_mlp[l], 'w_up': w_up[l],
              'w_down': w_down[l]}
        yp, (k1, v1, c1, s1, r1) = trunk_layer(yp, pos_p, lw, conv_zero, None, None, None, None)
        k_past = cache_sb_k[l][page_table].reshape(b_s, past, SB_HEADS, SB_HEAD_DIM)
        v_past = cache_sb_v[l][page_table].reshape(b_s, past, SB_HEADS, SB_HEAD_DIM)
        ys, (k2, v2, c2, s2, r2) = trunk_layer(ys, pos_s, lw, state_ssd_conv[l], state_ssd[l], state_ret[l],
                                               k_past, v_past)
        kp.append(k1); vp.append(v1); cp.append(c1); sp.append(s1); rp.append(r1)
        ks_.append(k2); vs_.append(v2); cs.append(c2); ss.append(s2); rs.append(r2)
    sb_k_prompt = jnp.stack(kp)
    sb_v_prompt = jnp.stack(vp)
    sb_k_sample = jnp.stack(ks_)
    sb_v_sample = jnp.stack(vs_)
    ssd_conv_prompt = jnp.stack(cp)
    ssd_conv_sample = jnp.stack(cs)
    ssd_state_prompt = jnp.stack(sp)
    ssd_state_sample = jnp.stack(ss)
    ret_state_prompt = jnp.stack(rp)
    ret_state_sample = jnp.stack(rs)
    return (yp, ys, sb_k_prompt, sb_v_prompt, sb_k_sample, sb_v_sample, ssd_conv_prompt, ssd_conv_sample,
            ssd_state_prompt, ssd_state_sample, ret_state_prompt, ret_state_sample)
```

```python
import functools

import numpy as np
import jax
import jax.numpy as jnp
from jax import lax
from jax.experimental import pallas as pl
from jax.experimental.pallas import tpu as pltpu

F32 = jnp.float32
BF16 = jnp.bfloat16

SSD_HEAD_DIM = 64
SSD_HEADS = 8
SSD_GROUPS = 2
SSD_STATE = 128
SSD_WIDTH = SSD_HEADS * SSD_HEAD_DIM
SSD_CONV = 4
SSD_CONV_DIM = SSD_WIDTH + 2 * SSD_GROUPS * SSD_STATE
SB_HEADS = 8
SB_HEAD_DIM = 64
SB_WIDTH = SB_HEADS * SB_HEAD_DIM
RET_HEADS = 4
RET_DK = 64
RET_DV = 128
RET_QK_WIDTH = RET_HEADS * RET_DK
RET_V_WIDTH = RET_HEADS * RET_DV
ROPE_BASE = 10000.0
N_BRANCH = 3
EPS = 1e-6
CHUNK = 128
PAGE_SIZE = 128

LANES = 128
VMEM_LIMIT = 56 * 1024 * 1024

_NT = (((1,), (1,)), ((), ()))
_TN = (((0,), (0,)), ((), ()))


def _cparams(*sem):
    return pltpu.CompilerParams(dimension_semantics=sem, vmem_limit_bytes=VMEM_LIMIT)


def _split_bf16(a, terms):
    parts, rem = [], a
    for _ in range(terms):
        p = rem.astype(BF16)
        parts.append(p)
        rem = rem - p.astype(F32)
    return parts


def _dot_rhs01(a, m01, terms):
    out = None
    for p in _split_bf16(a, terms):
        d = jnp.dot(p, m01, preferred_element_type=F32)
        out = d if out is None else out + d
    return out


def _dot_lhs01(m01, a, terms):
    out = None
    for p in _split_bf16(a, terms):
        d = jnp.dot(m01, p, preferred_element_type=F32)
        out = d if out is None else out + d
    return out


def _softplus(x):
    return jnp.maximum(x, 0.0) + jnp.log1p(jnp.exp(-jnp.abs(x)))


def _silu(x):
    return x * (1.0 / (1.0 + jnp.exp(-x)))


def _rms(x, w):
    return x * lax.rsqrt(jnp.mean(x * x, axis=-1, keepdims=True) + EPS) * w


def _full(shape):
    nd = len(shape)
    return pl.BlockSpec(shape, lambda *_: (0,) * nd)


_IN_COLS = dict(z=(0, 512), xbc=(512, 1536), sq=(1536, 2048), sv=(2048, 2560),
                rq=(2560, 2816), rk=(2816, 3072), rv=(3072, 3584), rg=(3584, 4096))


def _in_proj_kernel(x_ref, nw_ref, w_ref, wkvt_ref, wdtc_ref, wdtt_ref, qn_ref, kn_ref, seg_ref,
                    z_ref, xbc_ref, dtc_ref, dtt_ref, qb_ref, kt_ref, vt_ref, rq_ref, rk_ref, rv_ref, rg_ref,
                    *prompt_refs):
    hb = _rms(x_ref[...], nw_ref[...]).astype(BF16)
    tm = hb.shape[0]

    def proj(name):
        lo, hi = _IN_COLS[name]
        return jnp.dot(hb, w_ref[:, lo:hi], preferred_element_type=F32)

    z_ref[...] = proj("z")
    xbc_ref[...] = proj("xbc")
    dtc_ref[...] = jnp.dot(hb, wdtc_ref[...], preferred_element_type=F32)
    dtt_ref[...] = lax.dot_general(wdtt_ref[...], hb, _NT, preferred_element_type=F32)
    sq = proj("sq")
    ms = _dot_rhs01(sq * sq, seg_ref[...], 2) * (1.0 / SB_HEAD_DIM)
    qb_ref[...] = (sq * lax.rsqrt(ms + EPS) * qn_ref[...] * (SB_HEAD_DIM ** -0.5)).astype(BF16)
    kvt = lax.dot_general(wkvt_ref[...], hb, _NT, preferred_element_type=F32)
    k3 = kvt[:SB_WIDTH].reshape(SB_HEADS, SB_HEAD_DIM, tm)
    k3 = k3 * lax.rsqrt(jnp.mean(k3 * k3, axis=1, keepdims=True) + EPS)
    kt = k3.reshape(SB_WIDTH, tm) * kn_ref[...]
    kt_ref[...] = kt
    vt_ref[...] = kvt[SB_WIDTH:]
    rq_ref[...] = proj("rq")
    rk_ref[...] = proj("rk")
    rv_ref[...] = proj("rv")
    rg_ref[...] = proj("rg")
    if prompt_refs:
        kblk_ref, vb_ref = prompt_refs
        ktb = kt.astype(BF16)
        for p in range(SB_HEADS // 2):
            for j in range(tm // CHUNK):
                kblk_ref[p, j] = ktb[p * LANES:(p + 1) * LANES, j * CHUNK:(j + 1) * CHUNK]
        vb_ref[...] = proj("sv").astype(BF16)


def _in_proj(x, nw, w_main, w_kvt, w_dtc, w_dtt, qn, kn_col, seg, prompt):
    n, d = x.shape
    tm = min(n, 256)
    row = lambda c: pl.BlockSpec((tm, c), lambda i: (i, 0))
    colmajor = lambda r: pl.BlockSpec((r, tm), lambda i: (0, i))
    outs = [("z", (n, 512), F32, row(512)), ("xbc", (n, 1024), F32, row(1024)),
            ("dtc", (n, LANES), F32, row(LANES)), ("dtt", (16, n), F32, colmajor(16)),
            ("qb", (n, 512), BF16, row(512)), ("kt", (SB_WIDTH, n), F32, colmajor(SB_WIDTH)),
            ("vt", (SB_WIDTH, n), F32, colmajor(SB_WIDTH)),
            ("rq", (n, 256), F32, row(256)), ("rk", (n, 256), F32, row(256)),
            ("rv", (n, 512), F32, row(512)), ("rg", (n, 512), F32, row(512))]
    if prompt:
        npair, per = SB_HEADS // 2, tm // CHUNK
        outs += [("kblk", (npair, n // CHUNK, LANES, CHUNK), BF16,
                  pl.BlockSpec((npair, per, LANES, CHUNK), lambda i: (0, i, 0, 0))),
                 ("vb", (n, 512), BF16, row(512))]
    args = [x, nw, w_main, w_kvt, w_dtc, w_dtt, qn, kn_col, seg]
    res = pl.pallas_call(
        _in_proj_kernel,
        grid=(n // tm,),
        in_specs=[row(d)] + [_full(a.shape) for a in args[1:]],
        out_specs=[o[3] for o in outs],
        out_shape=[jax.ShapeDtypeStruct(o[1], o[2]) for o in outs],
        compiler_params=_cparams("arbitrary"),
    )(*args)
    return dict(zip([o[0] for o in outs], res))


def _ssd_prompt_kernel(z_ref, xbc_ref, dtc_ref, dtt_ref, cw_ref, cb_ref, dtb_row_ref, dtb_col_ref,
                       aneg_row_ref, aneg_col_ref, dskip_ref, nw_ref, tril_ref, triu_ref,
                       o_ref, st_out_ref, xbuf, st):
    i = pl.program_id(0)
    c = CHUNK

    @pl.when(i == 0)
    def _():
        xbuf[0:8, :] = jnp.zeros((8, SSD_CONV_DIM), F32)
        st[...] = jnp.zeros_like(st)

    xbuf[8:8 + c, :] = xbc_ref[...]
    acc = cb_ref[...] + cw_ref[SSD_CONV - 1:SSD_CONV, :] * xbuf[8:8 + c, :]
    for k in range(SSD_CONV - 1):
        off = 8 - (SSD_CONV - 1) + k
        acc = acc + cw_ref[k:k + 1, :] * xbuf[off:off + c, :]
    xbuf[0:8, :] = xbuf[c:c + 8, :]
    xc = _silu(acc)
    xs = xc[:, :SSD_WIDTH]

    dt_c = _softplus(dtc_ref[...] + dtb_row_ref[...])
    dt_t = _softplus(dtt_ref[...] + dtb_col_ref[...])
    cum_c = _dot_lhs01(tril_ref[...], dt_c * aneg_row_ref[...], 3)
    cum_t = _dot_rhs01(dt_t * aneg_col_ref[...], triu_ref[...], 3)

    row = lax.broadcasted_iota(jnp.int32, (c, c), 0)
    col = lax.broadcasted_iota(jnp.int32, (c, c), 1)
    causal = row >= col
    first_half = lax.broadcasted_iota(jnp.int32, (c, LANES), 1) < SSD_HEAD_DIM

    ys = []
    for pair in range(SSD_HEADS // 2):
        g = pair // 2
        h0, h1 = 2 * pair, 2 * pair + 1
        bm = xc[:, SSD_WIDTH + g * SSD_STATE:SSD_WIDTH + (g + 1) * SSD_STATE].astype(BF16)
        cm = xc[:, SSD_WIDTH + (SSD_GROUPS + g) * SSD_STATE:
                SSD_WIDTH + (SSD_GROUPS + g + 1) * SSD_STATE].astype(BF16)
        cb = lax.dot_general(cm, bm, _NT, preferred_element_type=F32)
        xs_p = xs[:, pair * LANES:(pair + 1) * LANES]
        cc0, cc1 = cum_c[:, h0:h0 + 1], cum_c[:, h1:h1 + 1]
        dt_p = jnp.where(first_half, dt_c[:, h0:h0 + 1], dt_c[:, h1:h1 + 1])
        cc_p = jnp.where(first_half, cc0, cc1)
        last_p = jnp.where(first_half[0:1, :], cum_c[c - 1:c, h0:h0 + 1], cum_c[c - 1:c, h1:h1 + 1])
        xdt = xs_p * dt_p
        xdt_b = xdt.astype(BF16)
        yd = []
        for h, cc in ((h0, cc0), (h1, cc1)):
            seg = cc - cum_t[h:h + 1, :]
            decay = jnp.exp(jnp.where(causal, seg, -jnp.inf))
            yd.append(jnp.dot((cb * decay).astype(BF16), xdt_b, preferred_element_type=F32))
        y_diag = jnp.where(first_half, yd[0], yd[1])
        ent = st[pair * LANES:(pair + 1) * LANES, :]
        y_off = lax.dot_general(cm, ent.astype(BF16), _NT, preferred_element_type=F32) * jnp.exp(cc_p)
        xw = (xdt * jnp.exp(last_p - cc_p)).astype(BF16)
        contrib = lax.dot_general(xw, bm, _TN, preferred_element_type=F32)
        dec0 = jnp.exp(cum_c[c - 1:c, h0:h0 + 1])
        dec1 = jnp.exp(cum_c[c - 1:c, h1:h1 + 1])
        upper = lax.broadcasted_iota(jnp.int32, (LANES, SSD_STATE), 0) < SSD_HEAD_DIM
        st[pair * LANES:(pair + 1) * LANES, :] = ent * jnp.where(upper, dec0, dec1) + contrib
        ys.append(y_diag + y_off + dskip_ref[:, pair * LANES:(pair + 1) * LANES] * xs_p)

    y = jnp.concatenate(ys, axis=-1)
    u = y * _silu(z_ref[...])
    gw = SSD_WIDTH // SSD_GROUPS
    outs = []
    for g in range(SSD_GROUPS):
        ug = u[:, g * gw:(g + 1) * gw]
        outs.append(ug * lax.rsqrt(jnp.mean(ug * ug, axis=-1, keepdims=True) + EPS))
    o_ref[...] = jnp.concatenate(outs, axis=-1) * nw_ref[...]
    st_out_ref[...] = st[...]


def _ssd_prompt(p, cw, cb, dtb_row, dtb_col, aneg_row, aneg_col, dskip, nw, tril, triu):
    n = p["z"].shape[0]
    c = CHUNK
    row = lambda w: pl.BlockSpec((c, w), lambda i: (i, 0))
    consts = [cw, cb, dtb_row, dtb_col, aneg_row, aneg_col, dskip, nw, tril, triu]
    o, st = pl.pallas_call(
        _ssd_prompt_kernel,
        grid=(n // c,),
        in_specs=[row(SSD_WIDTH), row(SSD_CONV_DIM), row(LANES), pl.BlockSpec((16, c), lambda i: (0, i))]
                 + [_full(a.shape) for a in consts],
        out_specs=[row(SSD_WIDTH), _full((SSD_WIDTH, SSD_STATE))],
        out_shape=[jax.ShapeDtypeStruct((n, SSD_WIDTH), F32),
                   jax.ShapeDtypeStruct((SSD_WIDTH, SSD_STATE), F32)],
        scratch_shapes=[pltpu.VMEM((c + 8, SSD_CONV_DIM), F32), pltpu.VMEM((SSD_WIDTH, SSD_STATE), F32)],
        compiler_params=_cparams("arbitrary"),
    )(p["z"], p["xbc"], p["dtc"], p["dtt"], *consts)
    return o, st


def _rotate(x, cosf, sinf):
    w = x.shape[-1]
    half = RET_DK // 2
    ahead = pltpu.roll(x, w - half, 1)
    behind = pltpu.roll(x, half, 1)
    lane = lax.broadcasted_iota(jnp.int32, x.shape, 1) % RET_DK
    return x * cosf + jnp.where(lane < half, ahead, behind) * sinf


def _ret_prompt_kernel(rq_ref, rk_ref, rv_ref, rg_ref, cos_ref, sin_ref, dmat_ref, toend_ref, fstart_ref,
                       cdec_ref, nw_ref, o_ref, st_out_ref, st):
    i = pl.program_id(0)
    c = CHUNK

    @pl.when(i == 0)
    def _():
        st[...] = jnp.zeros_like(st)

    cosf, sinf = cos_ref[...], sin_ref[...]
    q = _rotate(rq_ref[...], cosf, sinf)
    k = _rotate(rk_ref[...], cosf, sinf) * (RET_DK ** -0.5)
    kw = (k * toend_ref[...]).astype(BF16)
    kb = k.astype(BF16)
    first_half = lax.broadcasted_iota(jnp.int32, (c, LANES), 1) < RET_DK
    ys = []
    for h in range(RET_HEADS):
        pair, odd = h // 2, h % 2
        sl = slice(pair * LANES, (pair + 1) * LANES)
        qm = jnp.where(first_half != bool(odd), q[:, sl], 0.0).astype(BF16)
        v = rv_ref[:, h * RET_DV:(h + 1) * RET_DV].astype(BF16)
        scores = lax.dot_general(qm, kb[:, sl], _NT, preferred_element_type=F32) * dmat_ref[h]
        y = jnp.dot(scores.astype(BF16), v, preferred_element_type=F32)
        ent = st[pair * LANES:(pair + 1) * LANES, :]
        y = y + jnp.dot(qm, ent.astype(BF16), preferred_element_type=F32) * fstart_ref[:, h * RET_DV:(h + 1) * RET_DV]
        contrib = lax.dot_general(kw[:, sl], v, _TN, preferred_element_type=F32)
        r0 = h * RET_DK
        st[r0:r0 + RET_DK, :] = (st[r0:r0 + RET_DK, :] * cdec_ref[r0:r0 + RET_DK, :]
                                 + contrib[odd * RET_DK:(odd + 1) * RET_DK, :])
        yc = y - jnp.mean(y, axis=-1, keepdims=True)
        ys.append(yc * lax.rsqrt(jnp.mean(yc * yc, axis=-1, keepdims=True) + EPS))
    y = jnp.concatenate(ys, axis=-1) * nw_ref[...]
    o_ref[...] = _silu(rg_ref[...]) * y
    st_out_ref[...] = st[...]


def _ret_prompt(p, cosf, sinf, dmat, toend, fstart, cdec, nw):
    n = p["rq"].shape[0]
    c = CHUNK
    row = lambda w: pl.BlockSpec((c, w), lambda i: (i, 0))
    consts = [dmat, toend, fstart, cdec, nw]
    o, st = pl.pallas_call(
        _ret_prompt_kernel,
        grid=(n // c,),
        in_specs=[row(RET_QK_WIDTH), row(RET_QK_WIDTH), row(RET_V_WIDTH), row(RET_V_WIDTH),
                  row(RET_QK_WIDTH), row(RET_QK_WIDTH)] + [_full(a.shape) for a in consts],
        out_specs=[row(RET_V_WIDTH), _full((RET_QK_WIDTH, RET_DV))],
        out_shape=[jax.ShapeDtypeStruct((n, RET_V_WIDTH), F32),
                   jax.ShapeDtypeStruct((RET_QK_WIDTH, RET_DV), F32)],
        scratch_shapes=[pltpu.VMEM((RET_QK_WIDTH, RET_DV), F32)],
        compiler_params=_cparams("arbitrary"),
    )(p["rq"], p["rk"], p["rv"], p["rg"], cosf, sinf, *consts)
    return o, st


def _sb_tile(qm, kj, vj, bias, uo, carry, acc, causal):
    tk = kj.shape[1]
    z = jnp.dot(qm, kj, preferred_element_type=F32) + bias
    lp = jnp.log(1.0 + jnp.exp(-jnp.abs(z)))
    ls = jnp.minimum(z, 0.0) - lp
    l1m = ls - z
    if causal is not None:
        l1m = jnp.where(causal, l1m, 0.0)
    s = _dot_rhs01(l1m, uo, 2)
    a = jnp.exp(ls + s[:, :tk] + carry)
    if causal is not None:
        a = jnp.where(causal, a, 0.0)
    acc = acc + jnp.dot(a.astype(BF16), vj, preferred_element_type=F32)
    return carry + s[:, tk:], acc


def _sb_prompt_kernel(bias_ref, q_ref, k_ref, v_ref, uo_ref, o_ref):
    pair, i = pl.program_id(0), pl.program_id(1)
    tq = q_ref.shape[0]
    tk = CHUNK
    q = q_ref[...]
    uo = uo_ref[...]
    first_half = lax.broadcasted_iota(jnp.int32, (tq, LANES), 1) < SB_HEAD_DIM
    row = lax.broadcasted_iota(jnp.int32, (tq, tk), 0)
    col = lax.broadcasted_iota(jnp.int32, (tq, tk), 1)
    accs = []
    for odd in range(2):
        bias = bias_ref[2 * pair + odd]
        qm = jnp.where(first_half != bool(odd), q, jnp.zeros_like(q))
        carry = jnp.zeros((tq, tk), F32)
        acc = jnp.zeros((tq, LANES), F32)
        for d in reversed(range(tq // tk)):
            blk = i * (tq // tk) + d
            causal = col + d * tk < row
            carry, acc = _sb_tile(qm, k_ref[blk], v_ref[pl.ds(pl.multiple_of(blk * tk, tk), tk), :],
                                  bias, uo, carry, acc, causal)

        def body(t, ca):
            blk = i * (tq // tk) - 1 - t
            return _sb_tile(qm, k_ref[blk], v_ref[pl.ds(pl.multiple_of(blk * tk, tk), tk), :],
                            bias, uo, ca[0], ca[1], None)

        carry, acc = lax.fori_loop(0, i * (tq // tk), body, (carry, acc))
        accs.append(acc)
    o_ref[...] = jnp.where(first_half, accs[0], accs[1])


def _sb_prompt(qb, kblk, vb, bias, uo, tq):
    n = qb.shape[0]
    npair = SB_HEADS // 2
    return pl.pallas_call(
        _sb_prompt_kernel,
        grid_spec=pltpu.PrefetchScalarGridSpec(
            num_scalar_prefetch=1,
            grid=(npair, n // tq),
            in_specs=[pl.BlockSpec((tq, LANES), lambda p, i, b: (i, p)),
                      pl.BlockSpec((None,) + kblk.shape[1:], lambda p, i, b: (p, 0, 0, 0)),
                      pl.BlockSpec((n, LANES), lambda p, i, b: (0, p)),
                      pl.BlockSpec(uo.shape, lambda p, i, b: (0, 0))],
            out_specs=pl.BlockSpec((tq, LANES), lambda p, i, b: (i, p)),
        ),
        out_shape=jax.ShapeDtypeStruct((n, SB_WIDTH), F32),
        compiler_params=_cparams("arbitrary", "arbitrary"),
    )(bias, qb, kblk, vb, uo)


def _merge_kernel(x_ref, a_ref, b_ref, c_ref, nw_ref, wg_ref, bg_ref, wa_ref, wb_ref, wc_ref, wo_ref, o_ref):
    x = x_ref[...]
    d = x.shape[-1]
    hb = _rms(x, nw_ref[...]).astype(BF16)
    merged = None
    for j, (br_ref, w_ref) in enumerate(((a_ref, wa_ref), (b_ref, wb_ref), (c_ref, wc_ref))):
        gate = jax.nn.sigmoid(jnp.dot(hb, wg_ref[:, j * d:(j + 1) * d], preferred_element_type=F32)
                              + bg_ref[:, j * d:(j + 1) * d])
        t = gate * jnp.dot(br_ref[...].astype(BF16), w_ref[...], preferred_element_type=F32)
        merged = t if merged is None else merged + t
    o_ref[...] = x + jnp.dot(merged.astype(BF16), wo_ref[...], preferred_element_type=F32)


def _merge(x, o_ssd, o_sb, o_ret, nw, wg, bg, wa, wb, wc, wo):
    n, d = x.shape
    tm = min(n, 256)
    row = lambda w: pl.BlockSpec((tm, w), lambda i: (i, 0))
    consts = [nw, wg, bg, wa, wb, wc, wo]
    return pl.pallas_call(
        _merge_kernel,
        grid=(n // tm,),
        in_specs=[row(d), row(o_ssd.shape[1]), row(o_sb.shape[1]), row(o_ret.shape[1])]
                 + [_full(a.shape) for a in consts],
        out_specs=row(d),
        out_shape=jax.ShapeDtypeStruct((n, d), F32),
        compiler_params=_cparams("arbitrary"),
    )(x, o_ssd, o_sb, o_ret, *consts)


def _mlp_kernel(x_ref, nw_ref, wu_ref, wd_ref, o_ref):
    x = x_ref[...]
    hb = _rms(x, nw_ref[...]).astype(BF16)
    up = jnp.maximum(jnp.dot(hb, wu_ref[...], preferred_element_type=F32), 0.0)
    o_ref[...] = x + jnp.dot((up * up).astype(BF16), wd_ref[...], preferred_element_type=F32)


def _mlp(x, nw, wu, wd):
    n, d = x.shape
    tm = min(n, 256)
    row = pl.BlockSpec((tm, d), lambda i: (i, 0))
    return pl.pallas_call(
        _mlp_kernel,
        grid=(n // tm,),
        in_specs=[row, _full(nw.shape), _full(wu.shape), _full(wd.shape)],
        out_specs=row,
        out_shape=jax.ShapeDtypeStruct((n, d), F32),
        compiler_params=_cparams("arbitrary"),
    )(x, nw, wu, wd)


def _dec_prep_kernel(xbc_ref, cp_ref, cw_ref, cb_ref, dtc_ref, dtb_ref, aneg_ref, rq_ref, rk_ref,
                     cos_ref, sin_ref, xc_ref, dt_ref, da_ref, q_ref, k_ref):
    acc = cb_ref[...] + cw_ref[SSD_CONV - 1:SSD_CONV, :] * xbc_ref[...]
    for k in range(SSD_CONV - 1):
        acc = acc + cw_ref[k:k + 1, :] * cp_ref[k]
    xc_ref[...] = _silu(acc)
    dt = _softplus(dtc_ref[...] + dtb_ref[...])
    dt_ref[...] = dt
    da_ref[...] = jnp.exp(dt * aneg_ref[...])
    q_ref[...] = _rotate(rq_ref[...], cos_ref[...], sin_ref[...])
    k_ref[...] = _rotate(rk_ref[...], cos_ref[...], sin_ref[...]) * (RET_DK ** -0.5)


def _dec_prep(xbc, cp, cw, cb, dtc, dtb, aneg, rq, rk, cosr, sinr):
    b = xbc.shape[0]
    args = [xbc, cp, cw, cb, dtc, dtb, aneg, rq, rk, cosr, sinr]
    return pl.pallas_call(
        _dec_prep_kernel,
        grid=(1,),
        in_specs=[_full(a.shape) for a in args],
        out_specs=[_full((b, SSD_CONV_DIM)), _full((b, LANES)), _full((b, LANES)),
                   _full((b, RET_QK_WIDTH)), _full((b, RET_QK_WIDTH))],
        out_shape=[jax.ShapeDtypeStruct((b, SSD_CONV_DIM), F32), jax.ShapeDtypeStruct((b, LANES), F32),
                   jax.ShapeDtypeStruct((b, LANES), F32), jax.ShapeDtypeStruct((b, RET_QK_WIDTH), F32),
                   jax.ShapeDtypeStruct((b, RET_QK_WIDTH), F32)],
        compiler_params=_cparams("arbitrary"),
    )(*args)


def _dec_state_kernel(s_ref, xs_ref, dt_ref, da_ref, bh_ref, ch_ref, r_ref, kc_ref, qc_ref, vr_ref, gam_ref,
                      s_out_ref, y_ref, r_out_ref, yr_ref):
    s_new = s_ref[...] * da_ref[...] + (xs_ref[...] * dt_ref[...]) * bh_ref[...]
    s_out_ref[...] = s_new
    y_ref[...] = jnp.sum(s_new * ch_ref[...], axis=-1, keepdims=True)
    r_new = r_ref[...] * gam_ref[...] + kc_ref[...] * vr_ref[...]
    r_out_ref[...] = r_new
    yr_ref[...] = jnp.sum(qc_ref[...] * r_new, axis=-2, keepdims=True)


def _dec_state(s, xs_col, dt_col, da_col, bh, ch, r, k_col, q_col, v_row, gam, nb):
    b = s.shape[0] // SSD_HEADS
    hs, hr = nb * SSD_HEADS, nb * RET_HEADS
    blk = lambda rows, a, c: pl.BlockSpec((rows, a, c), lambda i: (i, 0, 0))
    return pl.pallas_call(
        _dec_state_kernel,
        grid=(b // nb,),
        in_specs=[blk(hs, SSD_HEAD_DIM, SSD_STATE), blk(hs, SSD_HEAD_DIM, 1), blk(hs, 1, 1), blk(hs, 1, 1),
                  blk(hs, 1, SSD_STATE), blk(hs, 1, SSD_STATE),
                  blk(hr, RET_DK, RET_DV), blk(hr, RET_DK, 1), blk(hr, RET_DK, 1), blk(hr, 1, RET_DV),
                  blk(hr, 1, 1)],
        out_specs=[blk(hs, SSD_HEAD_DIM, SSD_STATE), blk(hs, SSD_HEAD_DIM, 1),
                   blk(hr, RET_DK, RET_DV), blk(hr, 1, RET_DV)],
        out_shape=[jax.ShapeDtypeStruct(s.shape, F32), jax.ShapeDtypeStruct(xs_col.shape, F32),
                   jax.ShapeDtypeStruct(r.shape, F32), jax.ShapeDtypeStruct(v_row.shape, F32)],
        compiler_params=_cparams("arbitrary"),
    )(s, xs_col, dt_col, da_col, bh, ch, r, k_col, q_col, v_row, gam)


def _dec_post_kernel(y_ref, xs_ref, dskip_ref, z_ref, snw_ref, yr_ref, rg_ref, rnw_ref, o_ssd_ref, o_ret_ref):
    u = (y_ref[...] + dskip_ref[...] * xs_ref[...]) * _silu(z_ref[...])
    gw = SSD_WIDTH // SSD_GROUPS
    outs = []
    for g in range(SSD_GROUPS):
        ug = u[:, g * gw:(g + 1) * gw]
        outs.append(ug * lax.rsqrt(jnp.mean(ug * ug, axis=-1, keepdims=True) + EPS))
    o_ssd_ref[...] = jnp.concatenate(outs, axis=-1) * snw_ref[...]
    ys = []
    for h in range(RET_HEADS):
        y = yr_ref[:, h * RET_DV:(h + 1) * RET_DV]
        yc = y - jnp.mean(y, axis=-1, keepdims=True)
        ys.append(yc * lax.rsqrt(jnp.mean(yc * yc, axis=-1, keepdims=True) + EPS))
    o_ret_ref[...] = _silu(rg_ref[...]) * (jnp.concatenate(ys, axis=-1) * rnw_ref[...])


def _dec_post(y, xs, dskip, z, snw, yr, rg, rnw):
    b = y.shape[0]
    args = [y, xs, dskip, z, snw, yr, rg, rnw]
    return pl.pallas_call(
        _dec_post_kernel,
        grid=(1,),
        in_specs=[_full(a.shape) for a in args],
        out_specs=[_full((b, SSD_WIDTH)), _full((b, RET_V_WIDTH))],
        out_shape=[jax.ShapeDtypeStruct((b, SSD_WIDTH), F32), jax.ShapeDtypeStruct((b, RET_V_WIDTH), F32)],
        compiler_params=_cparams("arbitrary"),
    )(*args)


HEAD_ROWS = 16


def _sb_decode_kernel(pt_ref, bias_ref, qbdt_ref, knewt_ref, vnewt_ref, uo_ref, diag_ref, *rest, pages, past):
    k_refs, v_refs = rest[:pages], rest[pages:2 * pages]
    o_ref, acc, carry = rest[2 * pages], rest[2 * pages + 1], rest[2 * pages + 2]
    bi, c = pl.program_id(0), pl.program_id(1)
    bias = bias_ref[...]
    qbdt = qbdt_ref[0]
    uo = uo_ref[...]

    def page(kt, vt, visible):
        keys = kt.shape[1]
        z = jnp.dot(qbdt, kt.astype(BF16), preferred_element_type=F32) + bias
        lp = jnp.log(1.0 + jnp.exp(-jnp.abs(z)))
        ls = jnp.minimum(z, 0.0) - lp
        l1m = ls - z
        if visible is not None:
            l1m = jnp.where(visible, l1m, 0.0)
        s = _dot_rhs01(l1m, uo, 2)
        a = jnp.exp(ls + s[:, :keys] + carry[...])
        if visible is not None:
            a = jnp.where(visible, a, 0.0)
        acc[...] += lax.dot_general(a.astype(BF16), vt.astype(BF16), _NT, preferred_element_type=F32)
        carry[...] += s[:, keys:]

    @pl.when(c == 0)
    def _():
        acc[...] = jnp.zeros_like(acc)
        carry[...] = jnp.zeros_like(carry)
        lane = lax.broadcasted_iota(jnp.int32, (HEAD_ROWS, LANES), 1)
        key_pos = jnp.full((HEAD_ROWS, LANES), past, jnp.int32)
        page(knewt_ref[...], vnewt_ref[...], (lane == bi) & (key_pos < past))

    for k_ref, v_ref in zip(k_refs, v_refs):
        page(k_ref[...], v_ref[...], None)

    @pl.when(c == pl.num_programs(1) - 1)
    def _():
        o_ref[0] = jnp.sum(acc[...] * diag_ref[...], axis=0, keepdims=True)


def _sb_decode(page_table, cache_kt, cache_vt, layer, bias_col, qbdt, knewt, vnewt, uo, diag, pages):
    b = qbdt.shape[0]
    n_pages = page_table.shape[0] // b
    n_phys = cache_kt.shape[0] // 2
    nc = n_pages // pages
    base = layer * n_phys

    def page_spec(j):
        def imap(bi, c, pt):
            logical = (nc - 1 - c) * pages + (pages - 1 - j)
            return (base + pt[bi * n_pages + logical], 0, 0)
        return pl.BlockSpec((None, SB_WIDTH, PAGE_SIZE), imap)

    const = lambda a: pl.BlockSpec(a.shape, lambda bi, c, pt: (0,) * a.ndim)
    return pl.pallas_call(
        functools.partial(_sb_decode_kernel, pages=pages, past=n_pages * PAGE_SIZE),
        grid_spec=pltpu.PrefetchScalarGridSpec(
            num_scalar_prefetch=1,
            grid=(b, nc),
            in_specs=[const(bias_col), pl.BlockSpec((1,) + qbdt.shape[1:], lambda bi, c, pt: (bi, 0, 0)),
                      const(knewt), const(vnewt), const(uo), const(diag)]
                     + [page_spec(j) for j in range(pages)] * 2,
            out_specs=pl.BlockSpec((1, 1, SB_WIDTH), lambda bi, c, pt: (bi, 0, 0)),
            scratch_shapes=[pltpu.VMEM((HEAD_ROWS, SB_WIDTH), F32), pltpu.VMEM((HEAD_ROWS, LANES), F32)],
        ),
        out_shape=jax.ShapeDtypeStruct((b, 1, SB_WIDTH), F32),
        compiler_params=_cparams("arbitrary", "arbitrary"),
    )(page_table, bias_col, qbdt, knewt, vnewt, uo, diag, *([cache_kt] * pages), *([cache_vt] * pages))


def _rope_tables(pos):
    half = RET_DK // 2
    inv = ROPE_BASE ** (-jnp.arange(half, dtype=F32) / half)
    ang = pos.astype(F32)[:, None] * inv[None, :]
    cos, sin = jnp.cos(ang), jnp.sin(ang)
    cosf = jnp.tile(jnp.concatenate([cos, cos], axis=-1), (1, RET_HEADS))
    sinf = jnp.tile(jnp.concatenate([-sin, sin], axis=-1), (1, RET_HEADS))
    return cosf, sinf


def _ret_tables():
    log_gamma = jnp.log1p(-jnp.exp2(-5.0 - jnp.arange(RET_HEADS, dtype=F32)))
    idx = jnp.arange(CHUNK, dtype=F32)
    rel = idx[:, None] - idx[None, :]
    dmat = jnp.exp(jnp.where((rel >= 0)[None], rel[None] * log_gamma[:, None, None], -jnp.inf))
    to_end = jnp.exp((CHUNK - 1 - idx)[None, :] * log_gamma[:, None])
    from_start = jnp.exp((idx + 1.0)[None, :] * log_gamma[:, None])
    toend = jnp.repeat(to_end.T, RET_DK, axis=1)
    fstart = jnp.repeat(from_start.T, RET_DV, axis=1)
    cdec = jnp.broadcast_to(jnp.repeat(jnp.exp(CHUNK * log_gamma), RET_DK)[:, None], (RET_QK_WIDTH, RET_DV))
    return log_gamma, dmat, toend, fstart, cdec


def _np01(a):
    return jnp.asarray(np.asarray(a, np.float32), BF16)


def kernel(x_prompt, x_sample, cache_sb_k, cache_sb_v, page_table, state_ssd_conv, state_ssd, state_ret,
           norm_mix, w_in, conv_w, conv_b, dt_bias, a_log, d_skip, ssd_norm_w, sb_q_norm, sb_k_norm,
           sb_bias, ret_norm_w, w_gate, b_gate, w_br_ssd, w_br_sb, w_br_ret, w_out, norm_mlp, w_up, w_down):
    b_p, seq, d_model = x_prompt.shape
    b_s, t_new, _ = x_sample.shape
    depth = w_in.shape[0]
    n_pages = page_table.shape[1]
    past = n_pages * PAGE_SIZE
    assert b_p == 1 and t_new == 1, "one prompt sequence and one new token per sample sequence"
    assert seq % 256 == 0 and b_s % 8 == 0 and n_pages % 8 == 0

    ii = np.arange(CHUNK)
    tril = _np01(ii[:, None] >= ii[None, :])
    triu = _np01(ii[:, None] <= ii[None, :])
    uo = jnp.concatenate([_np01(ii[:, None] > ii[None, :]), _np01(np.ones((CHUNK, CHUNK)))], axis=1)
    hh = np.arange(SB_WIDTH) // SB_HEAD_DIM
    seg = _np01(hh[:, None] == hh[None, :])
    head_of = np.arange(HEAD_ROWS)[:, None] == hh[None, :]
    diag = jnp.asarray(head_of, F32)
    log_gamma, dmat, toend, fstart, cdec = _ret_tables()
    cos_p, sin_p = _rope_tables(jnp.arange(seq, dtype=jnp.int32))
    cos_s, sin_s = _rope_tables(past + jnp.arange(t_new, dtype=jnp.int32))
    gam = jnp.tile(jnp.exp(log_gamma), b_s).reshape(b_s * RET_HEADS, 1, 1)

    sizes = [SSD_WIDTH, SSD_CONV_DIM, SSD_HEADS, SB_WIDTH, SB_WIDTH, SB_WIDTH,
             RET_QK_WIDTH, RET_QK_WIDTH, RET_V_WIDTH, RET_V_WIDTH]
    offs = np.concatenate([[0], np.cumsum(sizes)])
    cols = {name: slice(int(offs[j]), int(offs[j + 1]))
            for j, name in enumerate(["z", "xbc", "dt", "sq", "sk", "sv", "rq", "rk", "rv", "rg"])}
    pt_flat = page_table.reshape(-1)
    page_view = lambda c: jnp.transpose(c, (0, 1, 3, 4, 2)).reshape(depth * c.shape[1], SB_WIDTH, PAGE_SIZE)
    ckt, cvt = page_view(cache_sb_k), page_view(cache_sb_v)

    xp = x_prompt.reshape(seq, d_model)
    xs = x_sample.reshape(b_s, d_model)
    outs = [[] for _ in range(10)]
    for l in range(depth):
        wl = w_in[l]
        w_main = jnp.concatenate([wl[:, cols[c]] for c in ("z", "xbc", "sq", "sv", "rq", "rk", "rv", "rg")],
                                 axis=1).astype(BF16)
        w_kvt = jnp.concatenate([wl[:, cols["sk"]], wl[:, cols["sv"]]], axis=1).T.astype(BF16)
        w_dtc = jnp.pad(wl[:, cols["dt"]], ((0, 0), (0, LANES - SSD_HEADS))).astype(BF16)
        w_dtt = w_dtc[:, :HEAD_ROWS].T
        nw = norm_mix[l][None, :]
        qn = jnp.tile(sb_q_norm[l], SB_HEADS)[None, :]
        kn_col = jnp.tile(sb_k_norm[l], SB_HEADS)[:, None]
        a_neg = -jnp.exp(a_log[l])
        pad_h = lambda v: jnp.pad(v, (0, LANES - SSD_HEADS))
        dtb_row, aneg_row = pad_h(dt_bias[l])[None, :], pad_h(a_neg)[None, :]
        dtb_col, aneg_col = pad_h(dt_bias[l])[:HEAD_ROWS, None], pad_h(a_neg)[:HEAD_ROWS, None]
        dskip = jnp.repeat(d_skip[l], SSD_HEAD_DIM)[None, :]
        snw, rnw = ssd_norm_w[l][None, :], ret_norm_w[l][None, :]
        cw, cb = conv_w[l], conv_b[l][None, :]
        wg, bg = w_gate[l].astype(BF16), b_gate[l][None, :]
        wa, wb, wc = w_br_ssd[l].astype(BF16), w_br_sb[l].astype(BF16), w_br_ret[l].astype(BF16)
        wo, wu, wd = w_out[l].astype(BF16), w_up[l].astype(BF16), w_down[l].astype(BF16)
        nm = norm_mlp[l][None, :]
        kv_out = lambda t, b, rows: jnp.transpose(t.reshape(SB_HEADS, SB_HEAD_DIM, b, rows), (2, 3, 0, 1))

        p = _in_proj(xp, nw, w_main, w_kvt, w_dtc, w_dtt, qn, kn_col, seg, True)
        o_ssd, st_ssd = _ssd_prompt(p, cw, cb, dtb_row, dtb_col, aneg_row, aneg_col, dskip, snw, tril, triu)
        o_sb = _sb_prompt(p["qb"], p["kblk"], p["vb"], sb_bias[l], uo, 256)
        o_ret, st_ret = _ret_prompt(p, cos_p, sin_p, dmat, toend, fstart, cdec, rnw)
        xp = _mlp(_merge(xp, o_ssd, o_sb, o_ret, nw, wg, bg, wa, wb, wc, wo), nm, wu, wd)
        outs[0].append(kv_out(p["kt"], b_p, seq))
        outs[1].append(kv_out(p["vt"], b_p, seq))
        outs[4].append(p["xbc"][seq - (SSD_CONV - 1):].reshape(b_p, SSD_CONV - 1, SSD_CONV_DIM))
        outs[6].append(st_ssd.reshape(b_p, SSD_HEADS, SSD_HEAD_DIM, SSD_STATE))
        outs[8].append(st_ret.reshape(b_p, RET_HEADS, RET_DK, RET_DV))

        s = _in_proj(xs, nw, w_main, w_kvt, w_dtc, w_dtt, qn, kn_col, seg, False)
        cp = jnp.swapaxes(state_ssd_conv[l], 0, 1)
        xc, dt, da, rq, rk = _dec_prep(s["xbc"], cp, cw, cb, s["dtc"], dtb_row, aneg_row, s["rq"], s["rk"],
                                       cos_s, sin_s)
        x_ssd = xc[:, :SSD_WIDTH]
        per_head = lambda m: jnp.repeat(m.reshape(b_s, SSD_GROUPS, 1, SSD_STATE), SSD_HEADS // SSD_GROUPS,
                                        axis=1).reshape(b_s * SSD_HEADS, 1, SSD_STATE)
        bh = per_head(xc[:, SSD_WIDTH:SSD_WIDTH + SSD_GROUPS * SSD_STATE])
        ch = per_head(xc[:, SSD_WIDTH + SSD_GROUPS * SSD_STATE:])
        s_new, y_col, r_new, yr = _dec_state(
            state_ssd[l].reshape(b_s * SSD_HEADS, SSD_HEAD_DIM, SSD_STATE),
            x_ssd.reshape(b_s * SSD_HEADS, SSD_HEAD_DIM, 1),
            dt[:, :SSD_HEADS].reshape(b_s * SSD_HEADS, 1, 1), da[:, :SSD_HEADS].reshape(b_s * SSD_HEADS, 1, 1),
            bh, ch,
            state_ret[l].reshape(b_s * RET_HEADS, RET_DK, RET_DV),
            rk.reshape(b_s * RET_HEADS, RET_DK, 1), rq.reshape(b_s * RET_HEADS, RET_DK, 1),
            s["rv"].reshape(b_s * RET_HEADS, 1, RET_DV), gam, 8)
        o_ssd_s, o_ret_s = _dec_post(y_col.reshape(b_s, SSD_WIDTH), x_ssd, dskip, s["z"], snw,
                                     yr.reshape(b_s, RET_V_WIDTH), s["rg"], rnw)
        qbdt = jnp.where(head_of[None], s["qb"][:, None, :], jnp.zeros((), BF16))
        bias_col = pad_h(sb_bias[l])[:HEAD_ROWS, None]
        pad_b = lambda t: jnp.pad(t, ((0, 0), (0, LANES - b_s)))
        o_sb_s = _sb_decode(pt_flat, ckt, cvt, l, bias_col, qbdt, pad_b(s["kt"]), pad_b(s["vt"]), uo, diag,
                            8).reshape(b_s, SB_WIDTH)
        xs = _mlp(_merge(xs, o_ssd_s, o_sb_s, o_ret_s, nw, wg, bg, wa, wb, wc, wo), nm, wu, wd)
        outs[2].append(kv_out(s["kt"], b_s, t_new))
        outs[3].append(kv_out(s["vt"], b_s, t_new))
        outs[5].append(jnp.concatenate([state_ssd_conv[l][:, 1:], s["xbc"][:, None, :]], axis=1))
        outs[7].append(s_new.reshape(b_s, SSD_HEADS, SSD_HEAD_DIM, SSD_STATE))
        outs[9].append(r_new.reshape(b_s, RET_HEADS, RET_DK, RET_DV))

    stacked = [jnp.stack(o) for o in outs]
    return (xp.reshape(b_p, seq, d_model), xs.reshape(b_s, t_new, d_model), *stacked)
```

```python
import functools

import numpy as np
import jax
import jax.numpy as jnp
from jax import lax
from jax.experimental import pallas as pl
from jax.experimental.pallas import tpu as pltpu

F32 = jnp.float32
BF16 = jnp.bfloat16

SSD_HEAD_DIM = 64
SSD_HEADS = 8
SSD_GROUPS = 2
SSD_STATE = 128
SSD_WIDTH = SSD_HEADS * SSD_HEAD_DIM
SSD_CONV = 4
SSD_CONV_DIM = SSD_WIDTH + 2 * SSD_GROUPS * SSD_STATE
SB_HEADS = 8
SB_HEAD_DIM = 64
SB_WIDTH = SB_HEADS * SB_HEAD_DIM
RET_HEADS = 4
RET_DK = 64
RET_DV = 128
RET_QK_WIDTH = RET_HEADS * RET_DK
RET_V_WIDTH = RET_HEADS * RET_DV
ROPE_BASE = 10000.0
N_BRANCH = 3
EPS = 1e-6
CHUNK = 128
PAGE_SIZE = 128
SB_KEYS = 256
SB_TQ = 512
DEC_PAGES = 16

LANES = 128
VMEM_LIMIT = 56 * 1024 * 1024

_NT = (((1,), (1,)), ((), ()))
_TN = (((0,), (0,)), ((), ()))


def _cparams(*sem):
    return pltpu.CompilerParams(dimension_semantics=sem, vmem_limit_bytes=VMEM_LIMIT)


def _split_bf16(a, terms):
    parts, rem = [], a
    for _ in range(terms):
        p = rem.astype(BF16)
        parts.append(p)
        rem = rem - p.astype(F32)
    return parts


def _dot_rhs01(a, m01, terms):
    out = None
    for p in _split_bf16(a, terms):
        d = jnp.dot(p, m01, preferred_element_type=F32)
        out = d if out is None else out + d
    return out


def _dot_lhs01(m01, a, terms):
    out = None
    for p in _split_bf16(a, terms):
        d = jnp.dot(m01, p, preferred_element_type=F32)
        out = d if out is None else out + d
    return out


def _softplus(x):
    return jnp.maximum(x, 0.0) + jnp.log1p(jnp.exp(-jnp.abs(x)))


def _silu(x):
    return x * (1.0 / (1.0 + jnp.exp(-x)))


def _rms(x, w):
    return x * lax.rsqrt(jnp.mean(x * x, axis=-1, keepdims=True) + EPS) * w


def _full(shape):
    nd = len(shape)
    return pl.BlockSpec(shape, lambda *_: (0,) * nd)


_IN_COLS = dict(z=(0, 512), xbc=(512, 1536), sq=(1536, 2048), sv=(2048, 2560),
                rq=(2560, 2816), rk=(2816, 3072), rv=(3072, 3584), rg=(3584, 4096))


def _in_proj_kernel(*refs, prompt):
    (x_ref, nw_ref, w_ref, wkvt_ref, wdtc_ref, wdtt_ref, qn_ref, kn_ref, seg_ref) = refs[:9]
    refs = refs[9:]
    if prompt:
        qbias_ref, refs = refs[0], refs[1:]
    (z_ref, xbc_ref, dtc_ref, dtt_ref, qb_ref, kt_ref, vt_ref, rq_ref, rk_ref, rv_ref, rg_ref) = refs[:11]
    hb = _rms(x_ref[...], nw_ref[...]).astype(BF16)
    tm = hb.shape[0]

    def proj(name):
        lo, hi = _IN_COLS[name]
        return jnp.dot(hb, w_ref[:, lo:hi], preferred_element_type=F32)

    z_ref[...] = proj("z")
    xbc_ref[...] = proj("xbc")
    dtc_ref[...] = jnp.dot(hb, wdtc_ref[...], preferred_element_type=F32)
    dtt_ref[...] = lax.dot_general(wdtt_ref[...], hb, _NT, preferred_element_type=F32)
    sq = proj("sq")
    ms = _dot_rhs01(sq * sq, seg_ref[...], 2) * (1.0 / SB_HEAD_DIM)
    qb = (sq * lax.rsqrt(ms + EPS) * qn_ref[...] * (SB_HEAD_DIM ** -0.5)).astype(BF16)
    qb_ref[...] = qb
    kvt = lax.dot_general(wkvt_ref[...], hb, _NT, preferred_element_type=F32)
    k3 = kvt[:SB_WIDTH].reshape(SB_HEADS, SB_HEAD_DIM, tm)
    k3 = k3 * lax.rsqrt(jnp.mean(k3 * k3, axis=1, keepdims=True) + EPS)
    kt = k3.reshape(SB_WIDTH, tm) * kn_ref[...]
    kt_ref[...] = kt
    vt_ref[...] = kvt[SB_WIDTH:]
    rq_ref[...] = proj("rq")
    rk_ref[...] = proj("rk")
    rv_ref[...] = proj("rv")
    rg_ref[...] = proj("rg")
    if prompt:
        qp_ref, kblk_ref, vb_ref = refs[11:]
        ktb = kt.astype(BF16)
        for p in range(SB_HEADS // 2):
            qp_ref[:, 2 * p * LANES:(2 * p + 1) * LANES] = qb[:, p * LANES:(p + 1) * LANES]
            qp_ref[:, (2 * p + 1) * LANES:(2 * p + 2) * LANES] = jnp.broadcast_to(
                qbias_ref[:, p * LANES:(p + 1) * LANES], (tm, LANES))
            for j in range(tm // SB_KEYS):
                kblk_ref[p, j] = ktb[p * LANES:(p + 1) * LANES, j * SB_KEYS:(j + 1) * SB_KEYS]
        vb_ref[...] = proj("sv").astype(BF16)


def _in_proj(x, nw, w_main, w_kvt, w_dtc, w_dtt, qn, kn_col, seg, qbias=None):
    n, d = x.shape
    tm = min(n, 256)
    prompt = qbias is not None
    row = lambda c: pl.BlockSpec((tm, c), lambda i: (i, 0))
    colmajor = lambda r: pl.BlockSpec((r, tm), lambda i: (0, i))
    outs = [("z", (n, 512), F32, row(512)), ("xbc", (n, 1024), F32, row(1024)),
            ("dtc", (n, LANES), F32, row(LANES)), ("dtt", (16, n), F32, colmajor(16)),
            ("qb", (n, 512), BF16, row(512)), ("kt", (SB_WIDTH, n), F32, colmajor(SB_WIDTH)),
            ("vt", (SB_WIDTH, n), F32, colmajor(SB_WIDTH)),
            ("rq", (n, 256), F32, row(256)), ("rk", (n, 256), F32, row(256)),
            ("rv", (n, 512), F32, row(512)), ("rg", (n, 512), F32, row(512))]
    args = [x, nw, w_main, w_kvt, w_dtc, w_dtt, qn, kn_col, seg]
    if prompt:
        npair, per = SB_HEADS // 2, tm // SB_KEYS
        outs += [("qp", (n, 2 * SB_WIDTH), BF16, row(2 * SB_WIDTH)),
                 ("kblk", (npair, n // SB_KEYS, LANES, SB_KEYS), BF16,
                  pl.BlockSpec((npair, per, LANES, SB_KEYS), lambda i: (0, i, 0, 0))),
                 ("vb", (n, 512), BF16, row(512))]
        args.append(qbias)
    res = pl.pallas_call(
        functools.partial(_in_proj_kernel, prompt=prompt),
        grid=(n // tm,),
        in_specs=[row(d)] + [_full(a.shape) for a in args[1:]],
        out_specs=[o[3] for o in outs],
        out_shape=[jax.ShapeDtypeStruct(o[1], o[2]) for o in outs],
        compiler_params=_cparams("arbitrary"),
    )(*args)
    return dict(zip([o[0] for o in outs], res))


def _ssd_prompt_kernel(z_ref, xbc_ref, dtc_ref, dtt_ref, cw_ref, cb_ref, dtb_row_ref, dtb_col_ref,
                       aneg_row_ref, aneg_col_ref, dskip_ref, nw_ref, tril_ref, triu_ref,
                       o_ref, st_out_ref, xbuf, st):
    i = pl.program_id(0)
    c = CHUNK

    @pl.when(i == 0)
    def _():
        xbuf[0:8, :] = jnp.zeros((8, SSD_CONV_DIM), F32)
        st[...] = jnp.zeros_like(st)

    xbuf[8:8 + c, :] = xbc_ref[...]
    acc = cb_ref[...] + cw_ref[SSD_CONV - 1:SSD_CONV, :] * xbuf[8:8 + c, :]
    for k in range(SSD_CONV - 1):
        off = 8 - (SSD_CONV - 1) + k
        acc = acc + cw_ref[k:k + 1, :] * xbuf[off:off + c, :]
    xbuf[0:8, :] = xbuf[c:c + 8, :]
    xc = _silu(acc)
    xs = xc[:, :SSD_WIDTH]

    dt_c = _softplus(dtc_ref[...] + dtb_row_ref[...])
    dt_t = _softplus(dtt_ref[...] + dtb_col_ref[...])
    cum_c = _dot_lhs01(tril_ref[...], dt_c * aneg_row_ref[...], 3)
    cum_t = _dot_rhs01(dt_t * aneg_col_ref[...], triu_ref[...], 3)

    row = lax.broadcasted_iota(jnp.int32, (c, c), 0)
    col = lax.broadcasted_iota(jnp.int32, (c, c), 1)
    causal = row >= col
    first_half = lax.broadcasted_iota(jnp.int32, (c, LANES), 1) < SSD_HEAD_DIM

    ys = []
    for pair in range(SSD_HEADS // 2):
        g = pair // 2
        h0, h1 = 2 * pair, 2 * pair + 1
        bm = xc[:, SSD_WIDTH + g * SSD_STATE:SSD_WIDTH + (g + 1) * SSD_STATE].astype(BF16)
        cm = xc[:, SSD_WIDTH + (SSD_GROUPS + g) * SSD_STATE:
                SSD_WIDTH + (SSD_GROUPS + g + 1) * SSD_STATE].astype(BF16)
        cb = lax.dot_general(cm, bm, _NT, preferred_element_type=F32)
        xs_p = xs[:, pair * LANES:(pair + 1) * LANES]
        cc0, cc1 = cum_c[:, h0:h0 + 1], cum_c[:, h1:h1 + 1]
        dt_p = jnp.where(first_half, dt_c[:, h0:h0 + 1], dt_c[:, h1:h1 + 1])
        cc_p = jnp.where(first_half, cc0, cc1)
        last_p = jnp.where(first_half[0:1, :], cum_c[c - 1:c, h0:h0 + 1], cum_c[c - 1:c, h1:h1 + 1])
        xdt = xs_p * dt_p
        xdt_b = xdt.astype(BF16)
        yd = []
        for h, cc in ((h0, cc0), (h1, cc1)):
            seg = cc - cum_t[h:h + 1, :]
            decay = jnp.exp(jnp.where(causal, seg, -jnp.inf))
            yd.append(jnp.dot((cb * decay).astype(BF16), xdt_b, preferred_element_type=F32))
        y_diag = jnp.where(first_half, yd[0], yd[1])
        ent = st[pair * LANES:(pair + 1) * LANES, :]
        y_off = lax.dot_general(cm, ent.astype(BF16), _NT, preferred_element_type=F32) * jnp.exp(cc_p)
        xw = (xdt * jnp.exp(last_p - cc_p)).astype(BF16)
        contrib = lax.dot_general(xw, bm, _TN, preferred_element_type=F32)
        dec0 = jnp.exp(cum_c[c - 1:c, h0:h0 + 1])
        dec1 = jnp.exp(cum_c[c - 1:c, h1:h1 + 1])
        upper = lax.broadcasted_iota(jnp.int32, (LANES, SSD_STATE), 0) < SSD_HEAD_DIM
        st[pair * LANES:(pair + 1) * LANES, :] = ent * jnp.where(upper, dec0, dec1) + contrib
        ys.append(y_diag + y_off + dskip_ref[:, pair * LANES:(pair + 1) * LANES] * xs_p)

    y = jnp.concatenate(ys, axis=-1)
    u = y * _silu(z_ref[...])
    gw = SSD_WIDTH // SSD_GROUPS
    outs = []
    for g in range(SSD_GROUPS):
        ug = u[:, g * gw:(g + 1) * gw]
        outs.append(ug * lax.rsqrt(jnp.mean(ug * ug, axis=-1, keepdims=True) + EPS))
    o_ref[...] = jnp.concatenate(outs, axis=-1) * nw_ref[...]
    st_out_ref[...] = st[...]


def _ssd_prompt(p, cw, cb, dtb_row, dtb_col, aneg_row, aneg_col, dskip, nw, tril, triu):
    n = p["z"].shape[0]
    c = CHUNK
    row = lambda w: pl.BlockSpec((c, w), lambda i: (i, 0))
    consts = [cw, cb, dtb_row, dtb_col, aneg_row, aneg_col, dskip, nw, tril, triu]
    o, st = pl.pallas_call(
        _ssd_prompt_kernel,
        grid=(n // c,),
        in_specs=[row(SSD_WIDTH), row(SSD_CONV_DIM), row(LANES), pl.BlockSpec((16, c), lambda i: (0, i))]
                 + [_full(a.shape) for a in consts],
        out_specs=[row(SSD_WIDTH), _full((SSD_WIDTH, SSD_STATE))],
        out_shape=[jax.ShapeDtypeStruct((n, SSD_WIDTH), F32),
                   jax.ShapeDtypeStruct((SSD_WIDTH, SSD_STATE), F32)],
        scratch_shapes=[pltpu.VMEM((c + 8, SSD_CONV_DIM), F32), pltpu.VMEM((SSD_WIDTH, SSD_STATE), F32)],
        compiler_params=_cparams("arbitrary"),
    )(p["z"], p["xbc"], p["dtc"], p["dtt"], *consts)
    return o, st


def _rotate(x, cosf, sinf):
    w = x.shape[-1]
    half = RET_DK // 2
    ahead = pltpu.roll(x, w - half, 1)
    behind = pltpu.roll(x, half, 1)
    lane = lax.broadcasted_iota(jnp.int32, x.shape, 1) % RET_DK
    return x * cosf + jnp.where(lane < half, ahead, behind) * sinf


def _ret_prompt_kernel(rq_ref, rk_ref, rv_ref, rg_ref, cos_ref, sin_ref, dmat_ref, toend_ref, fstart_ref,
                       cdec_ref, nw_ref, o_ref, st_out_ref, st):
    i = pl.program_id(0)
    c = CHUNK

    @pl.when(i == 0)
    def _():
        st[...] = jnp.zeros_like(st)

    cosf, sinf = cos_ref[...], sin_ref[...]
    q = _rotate(rq_ref[...], cosf, sinf)
    k = _rotate(rk_ref[...], cosf, sinf) * (RET_DK ** -0.5)
    kw = (k * toend_ref[...]).astype(BF16)
    kb = k.astype(BF16)
    first_half = lax.broadcasted_iota(jnp.int32, (c, LANES), 1) < RET_DK
    ys = []
    for h in range(RET_HEADS):
        pair, odd = h // 2, h % 2
        sl = slice(pair * LANES, (pair + 1) * LANES)
        qm = jnp.where(first_half != bool(odd), q[:, sl], 0.0).astype(BF16)
        v = rv_ref[:, h * RET_DV:(h + 1) * RET_DV].astype(BF16)
        scores = lax.dot_general(qm, kb[:, sl], _NT, preferred_element_type=F32) * dmat_ref[h]
        y = jnp.dot(scores.astype(BF16), v, preferred_element_type=F32)
        ent = st[pair * LANES:(pair + 1) * LANES, :]
        y = y + jnp.dot(qm, ent.astype(BF16), preferred_element_type=F32) * fstart_ref[:, h * RET_DV:(h + 1) * RET_DV]
        contrib = lax.dot_general(kw[:, sl], v, _TN, preferred_element_type=F32)
        r0 = h * RET_DK
        st[r0:r0 + RET_DK, :] = (st[r0:r0 + RET_DK, :] * cdec_ref[r0:r0 + RET_DK, :]
                                 + contrib[odd * RET_DK:(odd + 1) * RET_DK, :])
        yc = y - jnp.mean(y, axis=-1, keepdims=True)
        ys.append(yc * lax.rsqrt(jnp.mean(yc * yc, axis=-1, keepdims=True) + EPS))
    y = jnp.concatenate(ys, axis=-1) * nw_ref[...]
    o_ref[...] = _silu(rg_ref[...]) * y
    st_out_ref[...] = st[...]


def _ret_prompt(p, cosf, sinf, dmat, toend, fstart, cdec, nw):
    n = p["rq"].shape[0]
    c = CHUNK
    row = lambda w: pl.BlockSpec((c, w), lambda i: (i, 0))
    consts = [dmat, toend, fstart, cdec, nw]
    o, st = pl.pallas_call(
        _ret_prompt_kernel,
        grid=(n // c,),
        in_specs=[row(RET_QK_WIDTH), row(RET_QK_WIDTH), row(RET_V_WIDTH), row(RET_V_WIDTH),
                  row(RET_QK_WIDTH), row(RET_QK_WIDTH)] + [_full(a.shape) for a in consts],
        out_specs=[row(RET_V_WIDTH), _full((RET_QK_WIDTH, RET_DV))],
        out_shape=[jax.ShapeDtypeStruct((n, RET_V_WIDTH), F32),
                   jax.ShapeDtypeStruct((RET_QK_WIDTH, RET_DV), F32)],
        scratch_shapes=[pltpu.VMEM((RET_QK_WIDTH, RET_DV), F32)],
        compiler_params=_cparams("arbitrary"),
    )(p["rq"], p["rk"], p["rv"], p["rg"], cosf, sinf, *consts)
    return o, st


def _neg_abs(x):
    sign = jnp.uint32(0x80000000)
    return lax.bitcast_convert_type(lax.bitcast_convert_type(x, jnp.uint32) | sign, F32)


def _sb_block(qp, k2, v, tail, uneg, carries, acc, key_off):
    tq, keys = qp.shape[0], k2.shape[1]
    zk = jnp.zeros((SB_HEAD_DIM, keys), BF16)
    kb = jnp.concatenate([jnp.concatenate([k2[:SB_HEAD_DIM], zk], axis=1),
                          jnp.concatenate([zk, k2[SB_HEAD_DIM:]], axis=1), tail], axis=0)
    z2 = jnp.dot(qp, kb, preferred_element_type=F32)
    causal = None
    if key_off is not None:
        row = lax.broadcasted_iota(jnp.int32, (tq, keys), 0)
        col = lax.broadcasted_iota(jnp.int32, (tq, keys), 1)
        causal = col + key_off < row
    a_parts, new_carries = [], []
    for h in range(2):
        z = z2[:, h * keys:(h + 1) * keys]
        sp = jnp.maximum(z, 0.0) + jnp.log(1.0 + jnp.exp(_neg_abs(z)))
        if causal is not None:
            sp = jnp.where(causal, sp, 0.0)
        s = jnp.dot(sp.astype(BF16), uneg, preferred_element_type=F32)
        c = carries[h]
        a = jnp.exp(z + s + jnp.concatenate([c] * (keys // LANES), axis=1))
        if causal is not None:
            a = jnp.where(causal, a, 0.0)
        a_parts.append(a.astype(BF16))
        new_carries.append(c + jnp.broadcast_to(s[:, 0:1], c.shape))
    lane_lo = lax.broadcasted_iota(jnp.int32, v.shape, 1) < SB_HEAD_DIM
    zero = jnp.zeros_like(v)
    vbd = jnp.concatenate([jnp.where(lane_lo, v, zero), jnp.where(lane_lo, zero, v)], axis=0)
    acc = acc + jnp.dot(jnp.concatenate(a_parts, axis=1), vbd, preferred_element_type=F32)
    return new_carries, acc


def _sb_prompt_kernel(q_ref, k_ref, v_ref, tail_ref, uneg_ref, o_ref):
    i = pl.program_id(1)
    tq, keys = q_ref.shape[0], k_ref.shape[2]
    per_group = tq // keys
    qp, tail, uneg = q_ref[...], tail_ref[...], uneg_ref[...]

    def group(first_blk, state, diagonal):
        carries, acc = [state[0], state[1]], state[2]
        for d in reversed(range(per_group)):
            blk = first_blk + d
            v = v_ref[pl.ds(pl.multiple_of(blk * keys, keys), keys), :]
            carries, acc = _sb_block(qp, k_ref[blk], v, tail, uneg, carries, acc,
                                     d * keys if diagonal else None)
        return carries[0], carries[1], acc

    zero = jnp.zeros((tq, LANES), F32)
    state = group(i * per_group, (zero, zero, zero), True)
    state = lax.fori_loop(0, i, lambda t, st: group((i - 1 - t) * per_group, st, False), state)
    o_ref[...] = state[2]


def _sb_prompt(qp, kblk, vb, tail, uneg, tq):
    n = qp.shape[0]
    npair = SB_HEADS // 2
    return pl.pallas_call(
        _sb_prompt_kernel,
        grid=(npair, n // tq),
        in_specs=[pl.BlockSpec((tq, 2 * LANES), lambda p, i: (i, p)),
                  pl.BlockSpec((None,) + kblk.shape[1:], lambda p, i: (p, 0, 0, 0)),
                  pl.BlockSpec((n, LANES), lambda p, i: (0, p)),
                  pl.BlockSpec(tail.shape, lambda p, i: (0, 0)),
                  pl.BlockSpec(uneg.shape, lambda p, i: (0, 0))],
        out_specs=pl.BlockSpec((tq, LANES), lambda p, i: (i, p)),
        out_shape=jax.ShapeDtypeStruct((n, SB_WIDTH), F32),
        compiler_params=_cparams("arbitrary", "arbitrary"),
    )(qp, kblk, vb, tail, uneg)


def _merge_kernel(x_ref, a_ref, b_ref, c_ref, nw_ref, wg_ref, bg_ref, wa_ref, wb_ref, wc_ref, wo_ref, o_ref):
    x = x_ref[...]
    d = x.shape[-1]
    hb = _rms(x, nw_ref[...]).astype(BF16)
    merged = None
    for j, (br_ref, w_ref) in enumerate(((a_ref, wa_ref), (b_ref, wb_ref), (c_ref, wc_ref))):
        gate = jax.nn.sigmoid(jnp.dot(hb, wg_ref[:, j * d:(j + 1) * d], preferred_element_type=F32)
                              + bg_ref[:, j * d:(j + 1) * d])
        t = gate * jnp.dot(br_ref[...].astype(BF16), w_ref[...], preferred_element_type=F32)
        merged = t if merged is None else merged + t
    o_ref[...] = x + jnp.dot(merged.astype(BF16), wo_ref[...], preferred_element_type=F32)


def _merge(x, o_ssd, o_sb, o_ret, nw, wg, bg, wa, wb, wc, wo):
    n, d = x.shape
    tm = min(n, 256)
    row = lambda w: pl.BlockSpec((tm, w), lambda i: (i, 0))
    consts = [nw, wg, bg, wa, wb, wc, wo]
    return pl.pallas_call(
        _merge_kernel,
        grid=(n // tm,),
        in_specs=[row(d), row(o_ssd.shape[1]), row(o_sb.shape[1]), row(o_ret.shape[1])]
                 + [_full(a.shape) for a in consts],
        out_specs=row(d),
        out_shape=jax.ShapeDtypeStruct((n, d), F32),
        compiler_params=_cparams("arbitrary"),
    )(x, o_ssd, o_sb, o_ret, *consts)


def _mlp_kernel(x_ref, nw_ref, wu_ref, wd_ref, o_ref):
    x = x_ref[...]
    hb = _rms(x, nw_ref[...]).astype(BF16)
    up = jnp.maximum(jnp.dot(hb, wu_ref[...], preferred_element_type=F32), 0.0)
    o_ref[...] = x + jnp.dot((up * up).astype(BF16), wd_ref[...], preferred_element_type=F32)


def _mlp(x, nw, wu, wd):
    n, d = x.shape
    tm = min(n, 256)
    row = pl.BlockSpec((tm, d), lambda i: (i, 0))
    return pl.pallas_call(
        _mlp_kernel,
        grid=(n // tm,),
        in_specs=[row, _full(nw.shape), _full(wu.shape), _full(wd.shape)],
        out_specs=row,
        out_shape=jax.ShapeDtypeStruct((n, d), F32),
        compiler_params=_cparams("arbitrary"),
    )(x, nw, wu, wd)


def _dec_prep_kernel(xbc_ref, cp_ref, cw_ref, cb_ref, dtc_ref, dtb_ref, aneg_ref, rq_ref, rk_ref,
                     cos_ref, sin_ref, xc_ref, dt_ref, da_ref, q_ref, k_ref):
    acc = cb_ref[...] + cw_ref[SSD_CONV - 1:SSD_CONV, :] * xbc_ref[...]
    for k in range(SSD_CONV - 1):
        acc = acc + cw_ref[k:k + 1, :] * cp_ref[k]
    xc_ref[...] = _silu(acc)
    dt = _softplus(dtc_ref[...] + dtb_ref[...])
    dt_ref[...] = dt
    da_ref[...] = jnp.exp(dt * aneg_ref[...])
    q_ref[...] = _rotate(rq_ref[...], cos_ref[...], sin_ref[...])
    k_ref[...] = _rotate(rk_ref[...], cos_ref[...], sin_ref[...]) * (RET_DK ** -0.5)


def _dec_prep(xbc, cp, cw, cb, dtc, dtb, aneg, rq, rk, cosr, sinr):
    b = xbc.shape[0]
    args = [xbc, cp, cw, cb, dtc, dtb, aneg, rq, rk, cosr, sinr]
    return pl.pallas_call(
        _dec_prep_kernel,
        grid=(1,),
        in_specs=[_full(a.shape) for a in args],
        out_specs=[_full((b, SSD_CONV_DIM)), _full((b, LANES)), _full((b, LANES)),
                   _full((b, RET_QK_WIDTH)), _full((b, RET_QK_WIDTH))],
        out_shape=[jax.ShapeDtypeStruct((b, SSD_CONV_DIM), F32), jax.ShapeDtypeStruct((b, LANES), F32),
                   jax.ShapeDtypeStruct((b, LANES), F32), jax.ShapeDtypeStruct((b, RET_QK_WIDTH), F32),
                   jax.ShapeDtypeStruct((b, RET_QK_WIDTH), F32)],
        compiler_params=_cparams("arbitrary"),
    )(*args)


def _dec_state_kernel(s_ref, xs_ref, dt_ref, da_ref, bh_ref, ch_ref, r_ref, kc_ref, qc_ref, vr_ref, gam_ref,
                      s_out_ref, y_ref, r_out_ref, yr_ref):
    s_new = s_ref[...] * da_ref[...] + (xs_ref[...] * dt_ref[...]) * bh_ref[...]
    s_out_ref[...] = s_new
    y_ref[...] = jnp.sum(s_new * ch_ref[...], axis=-1, keepdims=True)
    r_new = r_ref[...] * gam_ref[...] + kc_ref[...] * vr_ref[...]
    r_out_ref[...] = r_new
    yr_ref[...] = jnp.sum(qc_ref[...] * r_new, axis=-2, keepdims=True)


def _dec_state(s, xs_col, dt_col, da_col, bh, ch, r, k_col, q_col, v_row, gam, nb):
    b = s.shape[0] // SSD_HEADS
    hs, hr = nb * SSD_HEADS, nb * RET_HEADS
    blk = lambda rows, a, c: pl.BlockSpec((rows, a, c), lambda i: (i, 0, 0))
    return pl.pallas_call(
        _dec_state_kernel,
        grid=(b // nb,),
        in_specs=[blk(hs, SSD_HEAD_DIM, SSD_STATE), blk(hs, SSD_HEAD_DIM, 1), blk(hs, 1, 1), blk(hs, 1, 1),
                  blk(hs, 1, SSD_STATE), blk(hs, 1, SSD_STATE),
                  blk(hr, RET_DK, RET_DV), blk(hr, RET_DK, 1), blk(hr, RET_DK, 1), blk(hr, 1, RET_DV),
                  blk(hr, 1, 1)],
        out_specs=[blk(hs, SSD_HEAD_DIM, SSD_STATE), blk(hs, SSD_HEAD_DIM, 1),
                   blk(hr, RET_DK, RET_DV), blk(hr, 1, RET_DV)],
        out_shape=[jax.ShapeDtypeStruct(s.shape, F32), jax.ShapeDtypeStruct(xs_col.shape, F32),
                   jax.ShapeDtypeStruct(r.shape, F32), jax.ShapeDtypeStruct(v_row.shape, F32)],
        compiler_params=_cparams("arbitrary"),
    )(s, xs_col, dt_col, da_col, bh, ch, r, k_col, q_col, v_row, gam)


def _dec_post_kernel(y_ref, xs_ref, dskip_ref, z_ref, snw_ref, yr_ref, rg_ref, rnw_ref, o_ssd_ref, o_ret_ref):
    u = (y_ref[...] + dskip_ref[...] * xs_ref[...]) * _silu(z_ref[...])
    gw = SSD_WIDTH // SSD_GROUPS
    outs = []
    for g in range(SSD_GROUPS):
        ug = u[:, g * gw:(g + 1) * gw]
        outs.append(ug * lax.rsqrt(jnp.mean(ug * ug, axis=-1, keepdims=True) + EPS))
    o_ssd_ref[...] = jnp.concatenate(outs, axis=-1) * snw_ref[...]
    ys = []
    for h in range(RET_HEADS):
        y = yr_ref[:, h * RET_DV:(h + 1) * RET_DV]
        yc = y - jnp.mean(y, axis=-1, keepdims=True)
        ys.append(yc * lax.rsqrt(jnp.mean(yc * yc, axis=-1, keepdims=True) + EPS))
    o_ret_ref[...] = _silu(rg_ref[...]) * (jnp.concatenate(ys, axis=-1) * rnw_ref[...])


def _dec_post(y, xs, dskip, z, snw, yr, rg, rnw):
    b = y.shape[0]
    args = [y, xs, dskip, z, snw, yr, rg, rnw]
    return pl.pallas_call(
        _dec_post_kernel,
        grid=(1,),
        in_specs=[_full(a.shape) for a in args],
        out_specs=[_full((b, SSD_WIDTH)), _full((b, RET_V_WIDTH))],
        out_shape=[jax.ShapeDtypeStruct((b, SSD_WIDTH), F32), jax.ShapeDtypeStruct((b, RET_V_WIDTH), F32)],
        compiler_params=_cparams("arbitrary"),
    )(*args)


HEAD_ROWS = 16


def _sb_decode_kernel(pt_ref, bias_ref, qbdt_ref, knewt_ref, vnewt_ref, uo_ref, diag_ref, *rest, pages, past):
    k_refs, v_refs = rest[:pages], rest[pages:2 * pages]
    o_ref, acc, carry = rest[2 * pages], rest[2 * pages + 1], rest[2 * pages + 2]
    bi, c = pl.program_id(0), pl.program_id(1)
    bias = bias_ref[...]
    qbdt = qbdt_ref[0]
    uneg = uo_ref[...]

    def sums(kt, visible):
        z = jnp.dot(qbdt, kt.astype(BF16), preferred_element_type=F32) + bias
        sp = jnp.maximum(z, 0.0) + jnp.log(1.0 + jnp.exp(_neg_abs(z)))
        if visible is not None:
            sp = jnp.where(visible, sp, 0.0)
        return z, _dot_rhs01(sp, uneg, 2)

    def weigh(z, s, carry_in, vt, visible):
        a = jnp.exp(z + s + carry_in)
        if visible is not None:
            a = jnp.where(visible, a, 0.0)
        out = lax.dot_general(a.astype(BF16), vt.astype(BF16), _NT, preferred_element_type=F32)
        return out, carry_in + jnp.broadcast_to(s[:, 0:1], carry_in.shape)

    @pl.when(c == 0)
    def _():
        lane = lax.broadcasted_iota(jnp.int32, (HEAD_ROWS, LANES), 1)
        key_pos = jnp.full((HEAD_ROWS, LANES), past, jnp.int32)
        visible = (lane == bi) & (key_pos < past)
        z, s = sums(knewt_ref[...], visible)
        out, cnew = weigh(z, s, jnp.zeros((HEAD_ROWS, LANES), F32), vnewt_ref[...], visible)
        acc[...] = out
        carry[...] = cnew

    zs = [sums(k_ref[...], None) for k_ref in k_refs]
    cur = carry[...]
    total = acc[...]
    for (z, s), v_ref in zip(zs, v_refs):
        out, cur = weigh(z, s, cur, v_ref[...], None)
        total = total + out
    acc[...] = total
    carry[...] = cur

    @pl.when(c == pl.num_programs(1) - 1)
    def _():
        o_ref[0] = jnp.sum(acc[...] * diag_ref[...], axis=0, keepdims=True)


def _sb_decode(page_table, cache_kt, cache_vt, layer, bias_col, qbdt, knewt, vnewt, uo, diag, pages):
    b = qbdt.shape[0]
    n_pages = page_table.shape[0] // b
    n_phys = cache_kt.shape[0] // 2
    nc = n_pages // pages
    base = layer * n_phys

    def page_spec(j):
        def imap(bi, c, pt):
            logical = (nc - 1 - c) * pages + (pages - 1 - j)
            return (base + pt[bi * n_pages + logical], 0, 0)
        return pl.BlockSpec((None, SB_WIDTH, PAGE_SIZE), imap)

    const = lambda a: pl.BlockSpec(a.shape, lambda bi, c, pt: (0,) * a.ndim)
    return pl.pallas_call(
        functools.partial(_sb_decode_kernel, pages=pages, past=n_pages * PAGE_SIZE),
        grid_spec=pltpu.PrefetchScalarGridSpec(
            num_scalar_prefetch=1,
            grid=(b, nc),
            in_specs=[const(bias_col), pl.BlockSpec((1,) + qbdt.shape[1:], lambda bi, c, pt: (bi, 0, 0)),
                      const(knewt), const(vnewt), const(uo), const(diag)]
                     + [page_spec(j) for j in range(pages)] * 2,
            out_specs=pl.BlockSpec((1, 1, SB_WIDTH), lambda bi, c, pt: (bi, 0, 0)),
            scratch_shapes=[pltpu.VMEM((HEAD_ROWS, SB_WIDTH), F32), pltpu.VMEM((HEAD_ROWS, LANES), F32)],
        ),
        out_shape=jax.ShapeDtypeStruct((b, 1, SB_WIDTH), F32),
        compiler_params=_cparams("arbitrary", "arbitrary"),
    )(page_table, bias_col, qbdt, knewt, vnewt, uo, diag, *([cache_kt] * pages), *([cache_vt] * pages))


def _rope_tables(pos):
    half = RET_DK // 2
    inv = ROPE_BASE ** (-jnp.arange(half, dtype=F32) / half)
    ang = pos.astype(F32)[:, None] * inv[None, :]
    cos, sin = jnp.cos(ang), jnp.sin(ang)
    cosf = jnp.tile(jnp.concatenate([cos, cos], axis=-1), (1, RET_HEADS))
    sinf = jnp.tile(jnp.concatenate([-sin, sin], axis=-1), (1, RET_HEADS))
    return cosf, sinf


def _ret_tables():
    log_gamma = jnp.log1p(-jnp.exp2(-5.0 - jnp.arange(RET_HEADS, dtype=F32)))
    idx = jnp.arange(CHUNK, dtype=F32)
    rel = idx[:, None] - idx[None, :]
    dmat = jnp.exp(jnp.where((rel >= 0)[None], rel[None] * log_gamma[:, None, None], -jnp.inf))
    to_end = jnp.exp((CHUNK - 1 - idx)[None, :] * log_gamma[:, None])
    from_start = jnp.exp((idx + 1.0)[None, :] * log_gamma[:, None])
    toend = jnp.repeat(to_end.T, RET_DK, axis=1)
    fstart = jnp.repeat(from_start.T, RET_DV, axis=1)
    cdec = jnp.broadcast_to(jnp.repeat(jnp.exp(CHUNK * log_gamma), RET_DK)[:, None], (RET_QK_WIDTH, RET_DV))
    return log_gamma, dmat, toend, fstart, cdec


def _np01(a):
    return jnp.asarray(np.asarray(a, np.float32), BF16)


def kernel(x_prompt, x_sample, cache_sb_k, cache_sb_v, page_table, state_ssd_conv, state_ssd, state_ret,
           norm_mix, w_in, conv_w, conv_b, dt_bias, a_log, d_skip, ssd_norm_w, sb_q_norm, sb_k_norm,
           sb_bias, ret_norm_w, w_gate, b_gate, w_br_ssd, w_br_sb, w_br_ret, w_out, norm_mlp, w_up, w_down):
    b_p, seq, d_model = x_prompt.shape
    b_s, t_new, _ = x_sample.shape
    depth = w_in.shape[0]
    n_pages = page_table.shape[1]
    past = n_pages * PAGE_SIZE
    assert b_p == 1 and t_new == 1, "one prompt sequence and one new token per sample sequence"
    assert seq % SB_TQ == 0 and b_s % 8 == 0 and b_s <= LANES and n_pages % DEC_PAGES == 0

    ii = np.arange(CHUNK)
    tril = _np01(ii[:, None] >= ii[None, :])
    triu = _np01(ii[:, None] <= ii[None, :])
    neg_suffix = lambda m: _np01(-(np.arange(m)[:, None] >= np.arange(m)[None, :]).astype(np.float32))
    uneg_page, uneg_blk = neg_suffix(PAGE_SIZE), neg_suffix(SB_KEYS)
    tail_np = np.zeros((LANES, 2 * SB_KEYS), np.float32)
    tail_np[0:2, :SB_KEYS] = 1.0
    tail_np[2:4, SB_KEYS:] = 1.0
    tail = _np01(tail_np)
    hh = np.arange(SB_WIDTH) // SB_HEAD_DIM
    seg = _np01(hh[:, None] == hh[None, :])
    head_of = np.arange(HEAD_ROWS)[:, None] == hh[None, :]
    diag = jnp.asarray(head_of, F32)
    log_gamma, dmat, toend, fstart, cdec = _ret_tables()
    cos_p, sin_p = _rope_tables(jnp.arange(seq, dtype=jnp.int32))
    cos_s, sin_s = _rope_tables(past + jnp.arange(t_new, dtype=jnp.int32))
    gam = jnp.tile(jnp.exp(log_gamma), b_s).reshape(b_s * RET_HEADS, 1, 1)

    sizes = [SSD_WIDTH, SSD_CONV_DIM, SSD_HEADS, SB_WIDTH, SB_WIDTH, SB_WIDTH,
             RET_QK_WIDTH, RET_QK_WIDTH, RET_V_WIDTH, RET_V_WIDTH]
    offs = np.concatenate([[0], np.cumsum(sizes)])
    cols = {name: slice(int(offs[j]), int(offs[j + 1]))
            for j, name in enumerate(["z", "xbc", "dt", "sq", "sk", "sv", "rq", "rk", "rv", "rg"])}
    pt_flat = page_table.reshape(-1)
    page_view = lambda c: jnp.transpose(c, (0, 1, 3, 4, 2)).reshape(depth * c.shape[1], SB_WIDTH, PAGE_SIZE)
    ckt, cvt = page_view(cache_sb_k), page_view(cache_sb_v)

    xp = x_prompt.reshape(seq, d_model)
    xs = x_sample.reshape(b_s, d_model)
    outs = [[] for _ in range(10)]
    for l in range(depth):
        wl = w_in[l]
        w_main = jnp.concatenate([wl[:, cols[c]] for c in ("z", "xbc", "sq", "sv", "rq", "rk", "rv", "rg")],
                                 axis=1).astype(BF16)
        w_kvt = jnp.concatenate([wl[:, cols["sk"]], wl[:, cols["sv"]]], axis=1).T.astype(BF16)
        w_dtc = jnp.pad(wl[:, cols["dt"]], ((0, 0), (0, LANES - SSD_HEADS))).astype(BF16)
        w_dtt = w_dtc[:, :HEAD_ROWS].T
        nw = norm_mix[l][None, :]
        qn = jnp.tile(sb_q_norm[l], SB_HEADS)[None, :]
        kn_col = jnp.tile(sb_k_norm[l], SB_HEADS)[:, None]
        a_neg = -jnp.exp(a_log[l])
        pad_h = lambda v: jnp.pad(v, (0, LANES - SSD_HEADS))
        dtb_row, aneg_row = pad_h(dt_bias[l])[None, :], pad_h(a_neg)[None, :]
        dtb_col, aneg_col = pad_h(dt_bias[l])[:HEAD_ROWS, None], pad_h(a_neg)[:HEAD_ROWS, None]
        dskip = jnp.repeat(d_skip[l], SSD_HEAD_DIM)[None, :]
        snw, rnw = ssd_norm_w[l][None, :], ret_norm_w[l][None, :]
        cw, cb = conv_w[l], conv_b[l][None, :]
        wg, bg = w_gate[l].astype(BF16), b_gate[l][None, :]
        wa, wb, wc = w_br_ssd[l].astype(BF16), w_br_sb[l].astype(BF16), w_br_ret[l].astype(BF16)
        wo, wu, wd = w_out[l].astype(BF16), w_up[l].astype(BF16), w_down[l].astype(BF16)
        nm = norm_mlp[l][None, :]
        kv_out = lambda t, b, rows: jnp.transpose(t.reshape(SB_HEADS, SB_HEAD_DIM, b, rows), (2, 3, 0, 1))

        b_hi = sb_bias[l].astype(BF16)
        b_lo = (sb_bias[l] - b_hi.astype(F32)).astype(BF16)
        qbias = jnp.pad(jnp.stack([b_hi, b_lo], axis=1).reshape(SB_HEADS // 2, 4),
                        ((0, 0), (0, LANES - 4))).reshape(1, SB_WIDTH)
        p = _in_proj(xp, nw, w_main, w_kvt, w_dtc, w_dtt, qn, kn_col, seg, qbias)
        o_ssd, st_ssd = _ssd_prompt(p, cw, cb, dtb_row, dtb_col, aneg_row, aneg_col, dskip, snw, tril, triu)
        o_sb = _sb_prompt(p["qp"], p["kblk"], p["vb"], tail, uneg_blk, SB_TQ)
        o_ret, st_ret = _ret_prompt(p, cos_p, sin_p, dmat, toend, fstart, cdec, rnw)
        xp = _mlp(_merge(xp, o_ssd, o_sb, o_ret, nw, wg, bg, wa, wb, wc, wo), nm, wu, wd)
        outs[0].append(kv_out(p["kt"], b_p, seq))
        outs[1].append(kv_out(p["vt"], b_p, seq))
        outs[4].append(p["xbc"][seq - (SSD_CONV - 1):].reshape(b_p, SSD_CONV - 1, SSD_CONV_DIM))
        outs[6].append(st_ssd.reshape(b_p, SSD_HEADS, SSD_HEAD_DIM, SSD_STATE))
        outs[8].append(st_ret.reshape(b_p, RET_HEADS, RET_DK, RET_DV))

        s = _in_proj(xs, nw, w_main, w_kvt, w_dtc, w_dtt, qn, kn_col, seg)
        cp = jnp.swapaxes(state_ssd_conv[l], 0, 1)
        xc, dt, da, rq, rk = _dec_prep(s["xbc"], cp, cw, cb, s["dtc"], dtb_row, aneg_row, s["rq"], s["rk"],
                                       cos_s, sin_s)
        x_ssd = xc[:, :SSD_WIDTH]
        per_head = lambda m: jnp.repeat(m.reshape(b_s, SSD_GROUPS, 1, SSD_STATE), SSD_HEADS // SSD_GROUPS,
                                        axis=1).reshape(b_s * SSD_HEADS, 1, SSD_STATE)
        bh = per_head(xc[:, SSD_WIDTH:SSD_WIDTH + SSD_GROUPS * SSD_STATE])
        ch = per_head(xc[:, SSD_WIDTH + SSD_GROUPS * SSD_STATE:])
        s_new, y_col, r_new, yr = _dec_state(
            state_ssd[l].reshape(b_s * SSD_HEADS, SSD_HEAD_DIM, SSD_STATE),
            x_ssd.reshape(b_s * SSD_HEADS, SSD_HEAD_DIM, 1),
            dt[:, :SSD_HEADS].reshape(b_s * SSD_HEADS, 1, 1), da[:, :SSD_HEADS].reshape(b_s * SSD_HEADS, 1, 1),
            bh, ch,
            state_ret[l].reshape(b_s * RET_HEADS, RET_DK, RET_DV),
            rk.reshape(b_s * RET_HEADS, RET_DK, 1), rq.reshape(b_s * RET_HEADS, RET_DK, 1),
            s["rv"].reshape(b_s * RET_HEADS, 1, RET_DV), gam, 8)
        o_ssd_s, o_ret_s = _dec_post(y_col.reshape(b_s, SSD_WIDTH), x_ssd, dskip, s["z"], snw,
                                     yr.reshape(b_s, RET_V_WIDTH), s["rg"], rnw)
        qbdt = jnp.where(head_of[None], s["qb"][:, None, :], jnp.zeros((), BF16))
        bias_col = pad_h(sb_bias[l])[:HEAD_ROWS, None]
        pad_b = lambda t: jnp.pad(t, ((0, 0), (0, LANES - b_s)))
        o_sb_s = _sb_decode(pt_flat, ckt, cvt, l, bias_col, qbdt, pad_b(s["kt"]), pad_b(s["vt"]), uneg_page, diag,
                            DEC_PAGES).reshape(b_s, SB_WIDTH)
        xs = _mlp(_merge(xs, o_ssd_s, o_sb_s, o_ret_s, nw, wg, bg, wa, wb, wc, wo), nm, wu, wd)
        outs[2].append(kv_out(s["kt"], b_s, t_new))
        outs[3].append(kv_out(s["vt"], b_s, t_new))
        outs[5].append(jnp.concatenate([state_ssd_conv[l][:, 1:], s["xbc"][:, None, :]], axis=1))
        outs[7].append(s_new.reshape(b_s, SSD_HEADS, SSD_HEAD_DIM, SSD_STATE))
        outs[9].append(r_new.reshape(b_s, RET_HEADS, RET_DK, RET_DV))

    stacked = [jnp.stack(o) for o in outs]
    return (xp.reshape(b_p, seq, d_model), xs.reshape(b_s, t_new, d_model), *stacked)
```

```python
import functools

import numpy as np
import jax
import jax.numpy as jnp
from jax import lax
from jax.experimental import pallas as pl
from jax.experimental.pallas import tpu as pltpu

F32 = jnp.float32
BF16 = jnp.bfloat16

SSD_HEAD_DIM = 64
SSD_HEADS = 8
SSD_GROUPS = 2
SSD_STATE = 128
SSD_WIDTH = SSD_HEADS * SSD_HEAD_DIM
SSD_CONV = 4
SSD_CONV_DIM = SSD_WIDTH + 2 * SSD_GROUPS * SSD_STATE
SB_HEADS = 8
SB_HEAD_DIM = 64
SB_WIDTH = SB_HEADS * SB_HEAD_DIM
RET_HEADS = 4
RET_DK = 64
RET_DV = 128
RET_QK_WIDTH = RET_HEADS * RET_DK
RET_V_WIDTH = RET_HEADS * RET_DV
ROPE_BASE = 10000.0
N_BRANCH = 3
EPS = 1e-6
LOG2E = 1.4426950408889634
CHUNK = 128
PAGE_SIZE = 128
SB_KEYS = 256
SB_TQ = 512
SB_ROWS = 256
SB_MAX_LOGIT2 = 126.0
DEC_PAGES = 16

LANES = 128
VMEM_LIMIT = 56 * 1024 * 1024

_NT = (((1,), (1,)), ((), ()))
_TN = (((0,), (0,)), ((), ()))


def _cparams(*sem):
    return pltpu.CompilerParams(dimension_semantics=sem, vmem_limit_bytes=VMEM_LIMIT)


def _split_bf16(a, terms):
    parts, rem = [], a
    for _ in range(terms):
        p = rem.astype(BF16)
        parts.append(p)
        rem = rem - p.astype(F32)
    return parts


def _dot_rhs01(a, m01, terms):
    out = None
    for p in _split_bf16(a, terms):
        d = jnp.dot(p, m01, preferred_element_type=F32)
        out = d if out is None else out + d
    return out


def _dot_lhs01(m01, a, terms):
    out = None
    for p in _split_bf16(a, terms):
        d = jnp.dot(m01, p, preferred_element_type=F32)
        out = d if out is None else out + d
    return out


def _softplus(x):
    return jnp.maximum(x, 0.0) + jnp.log1p(jnp.exp(-jnp.abs(x)))


def _silu(x):
    return x * (1.0 / (1.0 + jnp.exp(-x)))


def _rms(x, w):
    return x * lax.rsqrt(jnp.mean(x * x, axis=-1, keepdims=True) + EPS) * w


def _full(shape):
    nd = len(shape)
    return pl.BlockSpec(shape, lambda *_: (0,) * nd)


_IN_COLS = dict(z=(0, 512), xbc=(512, 1536), sq=(1536, 2048), sv=(2048, 2560),
                rq=(2560, 2816), rk=(2816, 3072), rv=(3072, 3584), rg=(3584, 4096))


def _in_proj_kernel(*refs, prompt):
    (x_ref, nw_ref, w_ref, wkvt_ref, wdtc_ref, wdtt_ref, qn_ref, kn_ref, seg_ref) = refs[:9]
    refs = refs[9:]
    if prompt:
        qbias_ref, refs = refs[0], refs[1:]
    (z_ref, xbc_ref, dtc_ref, dtt_ref, qb_ref, kt_ref, vt_ref, rq_ref, rk_ref, rv_ref, rg_ref) = refs[:11]
    hb = _rms(x_ref[...], nw_ref[...]).astype(BF16)
    tm = hb.shape[0]

    def proj(name):
        lo, hi = _IN_COLS[name]
        return jnp.dot(hb, w_ref[:, lo:hi], preferred_element_type=F32)

    z_ref[...] = proj("z")
    xbc_ref[...] = proj("xbc")
    dtc_ref[...] = jnp.dot(hb, wdtc_ref[...], preferred_element_type=F32)
    dtt_ref[...] = lax.dot_general(wdtt_ref[...], hb, _NT, preferred_element_type=F32)
    sq = proj("sq")
    ms = _dot_rhs01(sq * sq, seg_ref[...], 2) * (1.0 / SB_HEAD_DIM)
    qf = sq * lax.rsqrt(ms + EPS) * qn_ref[...] * (SB_HEAD_DIM ** -0.5)
    qb_ref[...] = qf.astype(BF16)
    kvt = lax.dot_general(wkvt_ref[...], hb, _NT, preferred_element_type=F32)
    k3 = kvt[:SB_WIDTH].reshape(SB_HEADS, SB_HEAD_DIM, tm)
    k3 = k3 * lax.rsqrt(jnp.mean(k3 * k3, axis=1, keepdims=True) + EPS)
    kt = k3.reshape(SB_WIDTH, tm) * kn_ref[...]
    kt_ref[...] = kt
    vt_ref[...] = kvt[SB_WIDTH:]
    rq_ref[...] = proj("rq")
    rk_ref[...] = proj("rk")
    rv_ref[...] = proj("rv")
    rg_ref[...] = proj("rg")
    if prompt:
        qp_ref, kblk_ref, vb_ref = refs[11:]
        ktb = kt.astype(BF16)
        qb = (qf * LOG2E).astype(BF16)
        for p in range(SB_HEADS // 2):
            qp_ref[:, 2 * p * LANES:(2 * p + 1) * LANES] = qb[:, p * LANES:(p + 1) * LANES]
            qp_ref[:, (2 * p + 1) * LANES:(2 * p + 2) * LANES] = jnp.broadcast_to(
                qbias_ref[:, p * LANES:(p + 1) * LANES], (tm, LANES))
            for j in range(tm // SB_KEYS):
                kblk_ref[p, j] = ktb[p * LANES:(p + 1) * LANES, j * SB_KEYS:(j + 1) * SB_KEYS]
        vb_ref[...] = proj("sv").astype(BF16)


def _in_proj(x, nw, w_main, w_kvt, w_dtc, w_dtt, qn, kn_col, seg, qbias=None):
    n, d = x.shape
    tm = min(n, 256)
    prompt = qbias is not None
    row = lambda c: pl.BlockSpec((tm, c), lambda i: (i, 0))
    colmajor = lambda r: pl.BlockSpec((r, tm), lambda i: (0, i))
    outs = [("z", (n, 512), F32, row(512)), ("xbc", (n, 1024), F32, row(1024)),
            ("dtc", (n, LANES), F32, row(LANES)), ("dtt", (16, n), F32, colmajor(16)),
            ("qb", (n, 512), BF16, row(512)), ("kt", (SB_WIDTH, n), F32, colmajor(SB_WIDTH)),
            ("vt", (SB_WIDTH, n), F32, colmajor(SB_WIDTH)),
            ("rq", (n, 256), F32, row(256)), ("rk", (n, 256), F32, row(256)),
            ("rv", (n, 512), F32, row(512)), ("rg", (n, 512), F32, row(512))]
    args = [x, nw, w_main, w_kvt, w_dtc, w_dtt, qn, kn_col, seg]
    if prompt:
        npair, per = SB_HEADS // 2, tm // SB_KEYS
        outs += [("qp", (n, 2 * SB_WIDTH), BF16, row(2 * SB_WIDTH)),
                 ("kblk", (npair, n // SB_KEYS, LANES, SB_KEYS), BF16,
                  pl.BlockSpec((npair, per, LANES, SB_KEYS), lambda i: (0, i, 0, 0))),
                 ("vb", (n, 512), BF16, row(512))]
        args.append(qbias)
    res = pl.pallas_call(
        functools.partial(_in_proj_kernel, prompt=prompt),
        grid=(n // tm,),
        in_specs=[row(d)] + [_full(a.shape) for a in args[1:]],
        out_specs=[o[3] for o in outs],
        out_shape=[jax.ShapeDtypeStruct(o[1], o[2]) for o in outs],
        compiler_params=_cparams("arbitrary"),
    )(*args)
    return dict(zip([o[0] for o in outs], res))


def _ssd_prompt_kernel(z_ref, xbc_ref, dtc_ref, dtt_ref, cw_ref, cb_ref, dtb_row_ref, dtb_col_ref,
                       aneg_row_ref, aneg_col_ref, dskip_ref, nw_ref, tril_ref, triu_ref,
                       o_ref, st_out_ref, xbuf, st):
    i = pl.program_id(0)
    c = CHUNK

    @pl.when(i == 0)
    def _():
        xbuf[0:8, :] = jnp.zeros((8, SSD_CONV_DIM), F32)
        st[...] = jnp.zeros_like(st)

    xbuf[8:8 + c, :] = xbc_ref[...]
    acc = cb_ref[...] + cw_ref[SSD_CONV - 1:SSD_CONV, :] * xbuf[8:8 + c, :]
    for k in range(SSD_CONV - 1):
        off = 8 - (SSD_CONV - 1) + k
        acc = acc + cw_ref[k:k + 1, :] * xbuf[off:off + c, :]
    xbuf[0:8, :] = xbuf[c:c + 8, :]
    xc = _silu(acc)
    xs = xc[:, :SSD_WIDTH]

    dt_c = _softplus(dtc_ref[...] + dtb_row_ref[...])
    dt_t = _softplus(dtt_ref[...] + dtb_col_ref[...])
    cum_c = _dot_lhs01(tril_ref[...], dt_c * aneg_row_ref[...], 3)
    cum_t = _dot_rhs01(dt_t * aneg_col_ref[...], triu_ref[...], 3)

    row = lax.broadcasted_iota(jnp.int32, (c, c), 0)
    col = lax.broadcasted_iota(jnp.int32, (c, c), 1)
    causal = row >= col
    first_half = lax.broadcasted_iota(jnp.int32, (c, LANES), 1) < SSD_HEAD_DIM

    ys = []
    for pair in range(SSD_HEADS // 2):
        g = pair // 2
        h0, h1 = 2 * pair, 2 * pair + 1
        bm = xc[:, SSD_WIDTH + g * SSD_STATE:SSD_WIDTH + (g + 1) * SSD_STATE].astype(BF16)
        cm = xc[:, SSD_WIDTH + (SSD_GROUPS + g) * SSD_STATE:
                SSD_WIDTH + (SSD_GROUPS + g + 1) * SSD_STATE].astype(BF16)
        cb = lax.dot_general(cm, bm, _NT, preferred_element_type=F32)
        xs_p = xs[:, pair * LANES:(pair + 1) * LANES]
        cc0, cc1 = cum_c[:, h0:h0 + 1], cum_c[:, h1:h1 + 1]
        dt_p = jnp.where(first_half, dt_c[:, h0:h0 + 1], dt_c[:, h1:h1 + 1])
        cc_p = jnp.where(first_half, cc0, cc1)
        last_p = jnp.where(first_half[0:1, :], cum_c[c - 1:c, h0:h0 + 1], cum_c[c - 1:c, h1:h1 + 1])
        xdt = xs_p * dt_p
        xdt_b = xdt.astype(BF16)
        yd = []
        for h, cc in ((h0, cc0), (h1, cc1)):
            seg = cc - cum_t[h:h + 1, :]
            decay = jnp.exp(jnp.where(causal, seg, -jnp.inf))
            yd.append(jnp.dot((cb * decay).astype(BF16), xdt_b, preferred_element_type=F32))
        y_diag = jnp.where(first_half, yd[0], yd[1])
        ent = st[pair * LANES:(pair + 1) * LANES, :]
        y_off = lax.dot_general(cm, ent.astype(BF16), _NT, preferred_element_type=F32) * jnp.exp(cc_p)
        xw = (xdt * jnp.exp(last_p - cc_p)).astype(BF16)
        contrib = lax.dot_general(xw, bm, _TN, preferred_element_type=F32)
        dec0 = jnp.exp(cum_c[c - 1:c, h0:h0 + 1])
        dec1 = jnp.exp(cum_c[c - 1:c, h1:h1 + 1])
        upper = lax.broadcasted_iota(jnp.int32, (LANES, SSD_STATE), 0) < SSD_HEAD_DIM
        st[pair * LANES:(pair + 1) * LANES, :] = ent * jnp.where(upper, dec0, dec1) + contrib
        ys.append(y_diag + y_off + dskip_ref[:, pair * LANES:(pair + 1) * LANES] * xs_p)

    y = jnp.concatenate(ys, axis=-1)
    u = y * _silu(z_ref[...])
    gw = SSD_WIDTH // SSD_GROUPS
    outs = []
    for g in range(SSD_GROUPS):
        ug = u[:, g * gw:(g + 1) * gw]
        outs.append(ug * lax.rsqrt(jnp.mean(ug * ug, axis=-1, keepdims=True) + EPS))
    o_ref[...] = jnp.concatenate(outs, axis=-1) * nw_ref[...]
    st_out_ref[...] = st[...]


def _ssd_prompt(p, cw, cb, dtb_row, dtb_col, aneg_row, aneg_col, dskip, nw, tril, triu):
    n = p["z"].shape[0]
    c = CHUNK
    row = lambda w: pl.BlockSpec((c, w), lambda i: (i, 0))
    consts = [cw, cb, dtb_row, dtb_col, aneg_row, aneg_col, dskip, nw, tril, triu]
    o, st = pl.pallas_call(
        _ssd_prompt_kernel,
        grid=(n // c,),
        in_specs=[row(SSD_WIDTH), row(SSD_CONV_DIM), row(LANES), pl.BlockSpec((16, c), lambda i: (0, i))]
                 + [_full(a.shape) for a in consts],
        out_specs=[row(SSD_WIDTH), _full((SSD_WIDTH, SSD_STATE))],
        out_shape=[jax.ShapeDtypeStruct((n, SSD_WIDTH), F32),
                   jax.ShapeDtypeStruct((SSD_WIDTH, SSD_STATE), F32)],
        scratch_shapes=[pltpu.VMEM((c + 8, SSD_CONV_DIM), F32), pltpu.VMEM((SSD_WIDTH, SSD_STATE), F32)],
        compiler_params=_cparams("arbitrary"),
    )(p["z"], p["xbc"], p["dtc"], p["dtt"], *consts)
    return o, st


def _rotate(x, cosf, sinf):
    w = x.shape[-1]
    half = RET_DK // 2
    ahead = pltpu.roll(x, w - half, 1)
    behind = pltpu.roll(x, half, 1)
    lane = lax.broadcasted_iota(jnp.int32, x.shape, 1) % RET_DK
    return x * cosf + jnp.where(lane < half, ahead, behind) * sinf


def _ret_prompt_kernel(rq_ref, rk_ref, rv_ref, rg_ref, cos_ref, sin_ref, dmat_ref, toend_ref, fstart_ref,
                       cdec_ref, nw_ref, o_ref, st_out_ref, st):
    i = pl.program_id(0)
    c = CHUNK

    @pl.when(i == 0)
    def _():
        st[...] = jnp.zeros_like(st)

    cosf, sinf = cos_ref[...], sin_ref[...]
    q = _rotate(rq_ref[...], cosf, sinf)
    k = _rotate(rk_ref[...], cosf, sinf) * (RET_DK ** -0.5)
    kw = (k * toend_ref[...]).astype(BF16)
    kb = k.astype(BF16)
    first_half = lax.broadcasted_iota(jnp.int32, (c, LANES), 1) < RET_DK
    ys = []
    for h in range(RET_HEADS):
        pair, odd = h // 2, h % 2
        sl = slice(pair * LANES, (pair + 1) * LANES)
        qm = jnp.where(first_half != bool(odd), q[:, sl], 0.0).astype(BF16)
        v = rv_ref[:, h * RET_DV:(h + 1) * RET_DV].astype(BF16)
        scores = lax.dot_general(qm, kb[:, sl], _NT, preferred_element_type=F32) * dmat_ref[h]
        y = jnp.dot(scores.astype(BF16), v, preferred_element_type=F32)
        ent = st[pair * LANES:(pair + 1) * LANES, :]
        y = y + jnp.dot(qm, ent.astype(BF16), preferred_element_type=F32) * fstart_ref[:, h * RET_DV:(h + 1) * RET_DV]
        contrib = lax.dot_general(kw[:, sl], v, _TN, preferred_element_type=F32)
        r0 = h * RET_DK
        st[r0:r0 + RET_DK, :] = (st[r0:r0 + RET_DK, :] * cdec_ref[r0:r0 + RET_DK, :]
                                 + contrib[odd * RET_DK:(odd + 1) * RET_DK, :])
        yc = y - jnp.mean(y, axis=-1, keepdims=True)
        ys.append(yc * lax.rsqrt(jnp.mean(yc * yc, axis=-1, keepdims=True) + EPS))
    y = jnp.concatenate(ys, axis=-1) * nw_ref[...]
    o_ref[...] = _silu(rg_ref[...]) * y
    st_out_ref[...] = st[...]


def _ret_prompt(p, cosf, sinf, dmat, toend, fstart, cdec, nw):
    n = p["rq"].shape[0]
    c = CHUNK
    row = lambda w: pl.BlockSpec((c, w), lambda i: (i, 0))
    consts = [dmat, toend, fstart, cdec, nw]
    o, st = pl.pallas_call(
        _ret_prompt_kernel,
        grid=(n // c,),
        in_specs=[row(RET_QK_WIDTH), row(RET_QK_WIDTH), row(RET_V_WIDTH), row(RET_V_WIDTH),
                  row(RET_QK_WIDTH), row(RET_QK_WIDTH)] + [_full(a.shape) for a in consts],
        out_specs=[row(RET_V_WIDTH), _full((RET_QK_WIDTH, RET_DV))],
        out_shape=[jax.ShapeDtypeStruct((n, RET_V_WIDTH), F32),
                   jax.ShapeDtypeStruct((RET_QK_WIDTH, RET_DV), F32)],
        scratch_shapes=[pltpu.VMEM((RET_QK_WIDTH, RET_DV), F32)],
        compiler_params=_cparams("arbitrary"),
    )(p["rq"], p["rk"], p["rv"], p["rg"], cosf, sinf, *consts)
    return o, st


def _neg_abs(x):
    sign = jnp.uint32(0x80000000)
    return lax.bitcast_convert_type(lax.bitcast_convert_type(x, jnp.uint32) | sign, F32)


def _sb_weights(z2, uneg, carries, key_off):
    tq, keys = z2.shape[0], z2.shape[1] // 2
    causal = None
    if key_off is not None:
        row = lax.broadcasted_iota(jnp.int32, (tq, keys), 0)
        col = lax.broadcasted_iota(jnp.int32, (tq, keys), 1)
        causal = col + key_off < row
    a_parts, new_carries = [], []
    for h in range(2):
        z = z2[:, h * keys:(h + 1) * keys]
        sp = jnp.log(1.0 + jnp.exp2(jnp.minimum(z, SB_MAX_LOGIT2))) * LOG2E
        if causal is not None:
            sp = jnp.where(causal, sp, 0.0)
        s = jnp.dot(sp.astype(BF16), uneg, preferred_element_type=F32)
        c = carries[h]
        a = jnp.exp2(z + s + jnp.concatenate([c] * (keys // LANES), axis=1))
        if causal is not None:
            a = jnp.where(causal, a, 0.0)
        a_parts.append(a.astype(BF16))
        new_carries.append(c + jnp.broadcast_to(s[:, 0:1], c.shape))
    return jnp.concatenate(a_parts, axis=1), new_carries


def _sb_prompt_kernel(q_ref, k_ref, v_ref, tail_ref, uneg_ref, o_ref, zbuf, abuf, cbuf, accbuf):
    i = pl.program_id(1)
    tq, keys = q_ref.shape[0], k_ref.shape[2]
    assert tq == 2 * keys
    qp, tail, uneg = q_ref[...], tail_ref[...], uneg_ref[...]
    zk = jnp.zeros((SB_HEAD_DIM, keys), BF16)

    def key_weights(blk):
        k2 = k_ref[blk]
        return jnp.concatenate([jnp.concatenate([k2[:SB_HEAD_DIM], zk], axis=1),
                                jnp.concatenate([zk, k2[SB_HEAD_DIM:]], axis=1), tail], axis=0)

    def value_weights(blk):
        v = v_ref[pl.ds(pl.multiple_of(blk * keys, keys), keys), :]
        lane_lo = lax.broadcasted_iota(jnp.int32, v.shape, 1) < SB_HEAD_DIM
        zero = jnp.zeros_like(v)
        return jnp.concatenate([jnp.where(lane_lo, v, zero), jnp.where(lane_lo, zero, v)], axis=0)

    def scores(blk):
        return jnp.dot(qp, key_weights(blk), preferred_element_type=F32)

    def weighted_values(a, vbd):
        return jnp.dot(a, vbd, preferred_element_type=F32)

    zero = jnp.zeros((tq, LANES), F32)
    hi, lo = 2 * i + 1, 2 * i
    a_hi, carries = _sb_weights(scores(hi), uneg, [zero, zero], keys)
    accbuf[...] = weighted_values(a_hi, value_weights(hi))
    a_lo, carries = _sb_weights(scores(lo), uneg, carries, 0)
    abuf[...] = a_lo
    cbuf[0], cbuf[1] = carries
    zbuf[...] = scores(jnp.maximum(lo - 1, 0))

    @pl.loop(0, i)
    def _(t):
        hi = 2 * (i - t) - 1
        lo = hi - 1
        kb_lo, kb_next = key_weights(lo), key_weights(jnp.maximum(lo - 1, 0))
        vb_prev, vb_hi = value_weights(lo + 2), value_weights(hi)
        for r in range(tq // SB_ROWS):
            rows = slice(r * SB_ROWS, (r + 1) * SB_ROWS)
            z_lo = jnp.dot(qp[rows], kb_lo, preferred_element_type=F32)
            a_hi, carries = _sb_weights(zbuf[rows, :], uneg, [cbuf[0, rows, :], cbuf[1, rows, :]], None)
            acc = weighted_values(abuf[rows, :], vb_prev) + weighted_values(a_hi, vb_hi)
            a_lo, carries = _sb_weights(z_lo, uneg, carries, None)
            abuf[rows, :] = a_lo
            cbuf[0, rows, :], cbuf[1, rows, :] = carries
            zbuf[rows, :] = jnp.dot(qp[rows], kb_next, preferred_element_type=F32)
            accbuf[rows, :] += acc

    o_ref[...] = accbuf[...] + weighted_values(abuf[...], value_weights(0))


def _sb_prompt(qp, kblk, vb, tail, uneg, tq):
    n = qp.shape[0]
    npair = SB_HEADS // 2
    return pl.pallas_call(
        _sb_prompt_kernel,
        grid=(npair, n // tq),
        in_specs=[pl.BlockSpec((tq, 2 * LANES), lambda p, i: (i, p)),
                  pl.BlockSpec((None,) + kblk.shape[1:], lambda p, i: (p, 0, 0, 0)),
                  pl.BlockSpec((n, LANES), lambda p, i: (0, p)),
                  pl.BlockSpec(tail.shape, lambda p, i: (0, 0)),
                  pl.BlockSpec(uneg.shape, lambda p, i: (0, 0))],
        out_specs=pl.BlockSpec((tq, LANES), lambda p, i: (i, p)),
        out_shape=jax.ShapeDtypeStruct((n, SB_WIDTH), F32),
        scratch_shapes=[pltpu.VMEM((tq, 2 * SB_KEYS), F32), pltpu.VMEM((tq, 2 * SB_KEYS), BF16),
                        pltpu.VMEM((2, tq, LANES), F32), pltpu.VMEM((tq, LANES), F32)],
        compiler_params=_cparams("arbitrary", "arbitrary"),
    )(qp, kblk, vb, tail, uneg)


def _merge_kernel(x_ref, a_ref, b_ref, c_ref, nw_ref, wg_ref, bg_ref, wa_ref, wb_ref, wc_ref, wo_ref, o_ref):
    x = x_ref[...]
    d = x.shape[-1]
    hb = _rms(x, nw_ref[...]).astype(BF16)
    merged = None
    for j, (br_ref, w_ref) in enumerate(((a_ref, wa_ref), (b_ref, wb_ref), (c_ref, wc_ref))):
        gate = jax.nn.sigmoid(jnp.dot(hb, wg_ref[:, j * d:(j + 1) * d], preferred_element_type=F32)
                              + bg_ref[:, j * d:(j + 1) * d])
        t = gate * jnp.dot(br_ref[...].astype(BF16), w_ref[...], preferred_element_type=F32)
        merged = t if merged is None else merged + t
    o_ref[...] = x + jnp.dot(merged.astype(BF16), wo_ref[...], preferred_element_type=F32)


def _merge(x, o_ssd, o_sb, o_ret, nw, wg, bg, wa, wb, wc, wo):
    n, d = x.shape
    tm = min(n, 256)
    row = lambda w: pl.BlockSpec((tm, w), lambda i: (i, 0))
    consts = [nw, wg, bg, wa, wb, wc, wo]
    return pl.pallas_call(
        _merge_kernel,
        grid=(n // tm,),
        in_specs=[row(d), row(o_ssd.shape[1]), row(o_sb.shape[1]), row(o_ret.shape[1])]
                 + [_full(a.shape) for a in consts],
        out_specs=row(d),
        out_shape=jax.ShapeDtypeStruct((n, d), F32),
        compiler_params=_cparams("arbitrary"),
    )(x, o_ssd, o_sb, o_ret, *consts)


def _mlp_kernel(x_ref, nw_ref, wu_ref, wd_ref, o_ref):
    x = x_ref[...]
    hb = _rms(x, nw_ref[...]).astype(BF16)
    up = jnp.maximum(jnp.dot(hb, wu_ref[...], preferred_element_type=F32), 0.0)
    o_ref[...] = x + jnp.dot((up * up).astype(BF16), wd_ref[...], preferred_element_type=F32)


def _mlp(x, nw, wu, wd):
    n, d = x.shape
    tm = min(n, 256)
    row = pl.BlockSpec((tm, d), lambda i: (i, 0))
    return pl.pallas_call(
        _mlp_kernel,
        grid=(n // tm,),
        in_specs=[row, _full(nw.shape), _full(wu.shape), _full(wd.shape)],
        out_specs=row,
        out_shape=jax.ShapeDtypeStruct((n, d), F32),
        compiler_params=_cparams("arbitrary"),
    )(x, nw, wu, wd)


def _dec_prep_kernel(xbc_ref, cp_ref, cw_ref, cb_ref, dtc_ref, dtb_ref, aneg_ref, rq_ref, rk_ref,
                     cos_ref, sin_ref, xc_ref, dt_ref, da_ref, q_ref, k_ref):
    acc = cb_ref[...] + cw_ref[SSD_CONV - 1:SSD_CONV, :] * xbc_ref[...]
    for k in range(SSD_CONV - 1):
        acc = acc + cw_ref[k:k + 1, :] * cp_ref[k]
    xc_ref[...] = _silu(acc)
    dt = _softplus(dtc_ref[...] + dtb_ref[...])
    dt_ref[...] = dt
    da_ref[...] = jnp.exp(dt * aneg_ref[...])
    q_ref[...] = _rotate(rq_ref[...], cos_ref[...], sin_ref[...])
    k_ref[...] = _rotate(rk_ref[...], cos_ref[...], sin_ref[...]) * (RET_DK ** -0.5)


def _dec_prep(xbc, cp, cw, cb, dtc, dtb, aneg, rq, rk, cosr, sinr):
    b = xbc.shape[0]
    args = [xbc, cp, cw, cb, dtc, dtb, aneg, rq, rk, cosr, sinr]
    return pl.pallas_call(
        _dec_prep_kernel,
        grid=(1,),
        in_specs=[_full(a.shape) for a in args],
        out_specs=[_full((b, SSD_CONV_DIM)), _full((b, LANES)), _full((b, LANES)),
                   _full((b, RET_QK_WIDTH)), _full((b, RET_QK_WIDTH))],
        out_shape=[jax.ShapeDtypeStruct((b, SSD_CONV_DIM), F32), jax.ShapeDtypeStruct((b, LANES), F32),
                   jax.ShapeDtypeStruct((b, LANES), F32), jax.ShapeDtypeStruct((b, RET_QK_WIDTH), F32),
                   jax.ShapeDtypeStruct((b, RET_QK_WIDTH), F32)],
        compiler_params=_cparams("arbitrary"),
    )(*args)


def _dec_state_kernel(s_ref, xs_ref, dt_ref, da_ref, bh_ref, ch_ref, r_ref, kc_ref, qc_ref, vr_ref, gam_ref,
                      s_out_ref, y_ref, r_out_ref, yr_ref):
    s_new = s_ref[...] * da_ref[...] + (xs_ref[...] * dt_ref[...]) * bh_ref[...]
    s_out_ref[...] = s_new
    y_ref[...] = jnp.sum(s_new * ch_ref[...], axis=-1, keepdims=True)
    r_new = r_ref[...] * gam_ref[...] + kc_ref[...] * vr_ref[...]
    r_out_ref[...] = r_new
    yr_ref[...] = jnp.sum(qc_ref[...] * r_new, axis=-2, keepdims=True)


def _dec_state(s, xs_col, dt_col, da_col, bh, ch, r, k_col, q_col, v_row, gam, nb):
    b = s.shape[0] // SSD_HEADS
    hs, hr = nb * SSD_HEADS, nb * RET_HEADS
    blk = lambda rows, a, c: pl.BlockSpec((rows, a, c), lambda i: (i, 0, 0))
    return pl.pallas_call(
        _dec_state_kernel,
        grid=(b // nb,),
        in_specs=[blk(hs, SSD_HEAD_DIM, SSD_STATE), blk(hs, SSD_HEAD_DIM, 1), blk(hs, 1, 1), blk(hs, 1, 1),
                  blk(hs, 1, SSD_STATE), blk(hs, 1, SSD_STATE),
                  blk(hr, RET_DK, RET_DV), blk(hr, RET_DK, 1), blk(hr, RET_DK, 1), blk(hr, 1, RET_DV),
                  blk(hr, 1, 1)],
        out_specs=[blk(hs, SSD_HEAD_DIM, SSD_STATE), blk(hs, SSD_HEAD_DIM, 1),
                   blk(hr, RET_DK, RET_DV), blk(hr, 1, RET_DV)],
        out_shape=[jax.ShapeDtypeStruct(s.shape, F32), jax.ShapeDtypeStruct(xs_col.shape, F32),
                   jax.ShapeDtypeStruct(r.shape, F32), jax.ShapeDtypeStruct(v_row.shape, F32)],
        compiler_params=_cparams("arbitrary"),
    )(s, xs_col, dt_col, da_col, bh, ch, r, k_col, q_col, v_row, gam)


def _dec_post_kernel(y_ref, xs_ref, dskip_ref, z_ref, snw_ref, yr_ref, rg_ref, rnw_ref, o_ssd_ref, o_ret_ref):
    u = (y_ref[...] + dskip_ref[...] * xs_ref[...]) * _silu(z_ref[...])
    gw = SSD_WIDTH // SSD_GROUPS
    outs = []
    for g in range(SSD_GROUPS):
        ug = u[:, g * gw:(g + 1) * gw]
        outs.append(ug * lax.rsqrt(jnp.mean(ug * ug, axis=-1, keepdims=True) + EPS))
    o_ssd_ref[...] = jnp.concatenate(outs, axis=-1) * snw_ref[...]
    ys = []
    for h in range(RET_HEADS):
        y = yr_ref[:, h * RET_DV:(h + 1) * RET_DV]
        yc = y - jnp.mean(y, axis=-1, keepdims=True)
        ys.append(yc * lax.rsqrt(jnp.mean(yc * yc, axis=-1, keepdims=True) + EPS))
    o_ret_ref[...] = _silu(rg_ref[...]) * (jnp.concatenate(ys, axis=-1) * rnw_ref[...])


def _dec_post(y, xs, dskip, z, snw, yr, rg, rnw):
    b = y.shape[0]
    args = [y, xs, dskip, z, snw, yr, rg, rnw]
    return pl.pallas_call(
        _dec_post_kernel,
        grid=(1,),
        in_specs=[_full(a.shape) for a in args],
        out_specs=[_full((b, SSD_WIDTH)), _full((b, RET_V_WIDTH))],
        out_shape=[jax.ShapeDtypeStruct((b, SSD_WIDTH), F32), jax.ShapeDtypeStruct((b, RET_V_WIDTH), F32)],
        compiler_params=_cparams("arbitrary"),
    )(*args)


HEAD_ROWS = 16


def _sb_decode_kernel(pt_ref, bias_ref, qbdt_ref, knewt_ref, vnewt_ref, uo_ref, diag_ref, *rest, pages, past):
    k_refs, v_refs = rest[:pages], rest[pages:2 * pages]
    o_ref, acc, carry = rest[2 * pages], rest[2 * pages + 1], rest[2 * pages + 2]
    bi, c = pl.program_id(0), pl.program_id(1)
    bias = bias_ref[...]
    qbdt = qbdt_ref[0]
    uneg = uo_ref[...]

    def sums(kt, visible):
        z = jnp.dot(qbdt, kt.astype(BF16), preferred_element_type=F32) + bias
        sp = jnp.maximum(z, 0.0) + jnp.log(1.0 + jnp.exp(_neg_abs(z)))
        if visible is not None:
            sp = jnp.where(visible, sp, 0.0)
        return z, _dot_rhs01(sp, uneg, 2)

    def weigh(z, s, carry_in, vt, visible):
        a = jnp.exp(z + s + carry_in)
        if visible is not None:
            a = jnp.where(visible, a, 0.0)
        out = lax.dot_general(a.astype(BF16), vt.astype(BF16), _NT, preferred_element_type=F32)
        return out, carry_in + jnp.broadcast_to(s[:, 0:1], carry_in.shape)

    @pl.when(c == 0)
    def _():
        lane = lax.broadcasted_iota(jnp.int32, (HEAD_ROWS, LANES), 1)
        key_pos = jnp.full((HEAD_ROWS, LANES), past, jnp.int32)
        visible = (lane == bi) & (key_pos < past)
        z, s = sums(knewt_ref[...], visible)
        out, cnew = weigh(z, s, jnp.zeros((HEAD_ROWS, LANES), F32), vnewt_ref[...], visible)
        acc[...] = out
        carry[...] = cnew

    zs = [sums(k_ref[...], None) for k_ref in k_refs]
    cur = carry[...]
    total = acc[...]
    for (z, s), v_ref in zip(zs, v_refs):
        out, cur = weigh(z, s, cur, v_ref[...], None)
        total = total + out
    acc[...] = total
    carry[...] = cur

    @pl.when(c == pl.num_programs(1) - 1)
    def _():
        o_ref[0] = jnp.sum(acc[...] * diag_ref[...], axis=0, keepdims=True)


def _sb_decode(page_table, cache_kt, cache_vt, layer, bias_col, qbdt, knewt, vnewt, uo, diag, pages):
    b = qbdt.shape[0]
    n_pages = page_table.shape[0] // b
    n_phys = cache_kt.shape[0] // 2
    nc = n_pages // pages
    base = layer * n_phys

    def page_spec(j):
        def imap(bi, c, pt):
            logical = (nc - 1 - c) * pages + (pages - 1 - j)
            return (base + pt[bi * n_pages + logical], 0, 0)
        return pl.BlockSpec((None, SB_WIDTH, PAGE_SIZE), imap)

    const = lambda a: pl.BlockSpec(a.shape, lambda bi, c, pt: (0,) * a.ndim)
    return pl.pallas_call(
        functools.partial(_sb_decode_kernel, pages=pages, past=n_pages * PAGE_SIZE),
        grid_spec=pltpu.PrefetchScalarGridSpec(
            num_scalar_prefetch=1,
            grid=(b, nc),
            in_specs=[const(bias_col), pl.BlockSpec((1,) + qbdt.shape[1:], lambda bi, c, pt: (bi, 0, 0)),
                      const(knewt), const(vnewt), const(uo), const(diag)]
                     + [page_spec(j) for j in range(pages)] * 2,
            out_specs=pl.BlockSpec((1, 1, SB_WIDTH), lambda bi, c, pt: (bi, 0, 0)),
            scratch_shapes=[pltpu.VMEM((HEAD_ROWS, SB_WIDTH), F32), pltpu.VMEM((HEAD_ROWS, LANES), F32)],
        ),
        out_shape=jax.ShapeDtypeStruct((b, 1, SB_WIDTH), F32),
        compiler_params=_cparams("arbitrary", "arbitrary"),
    )(page_table, bias_col, qbdt, knewt, vnewt, uo, diag, *([cache_kt] * pages), *([cache_vt] * pages))


def _rope_tables(pos):
    half = RET_DK // 2
    inv = ROPE_BASE ** (-jnp.arange(half, dtype=F32) / half)
    ang = pos.astype(F32)[:, None] * inv[None, :]
    cos, sin = jnp.cos(ang), jnp.sin(ang)
    cosf = jnp.tile(jnp.concatenate([cos, cos], axis=-1), (1, RET_HEADS))
    sinf = jnp.tile(jnp.concatenate([-sin, sin], axis=-1), (1, RET_HEADS))
    return cosf, sinf


def _ret_tables():
    log_gamma = jnp.log1p(-jnp.exp2(-5.0 - jnp.arange(RET_HEADS, dtype=F32)))
    idx = jnp.arange(CHUNK, dtype=F32)
    rel = idx[:, None] - idx[None, :]
    dmat = jnp.exp(jnp.where((rel >= 0)[None], rel[None] * log_gamma[:, None, None], -jnp.inf))
    to_end = jnp.exp((CHUNK - 1 - idx)[None, :] * log_gamma[:, None])
    from_start = jnp.exp((idx + 1.0)[None, :] * log_gamma[:, None])
    toend = jnp.repeat(to_end.T, RET_DK, axis=1)
    fstart = jnp.repeat(from_start.T, RET_DV, axis=1)
    cdec = jnp.broadcast_to(jnp.repeat(jnp.exp(CHUNK * log_gamma), RET_DK)[:, None], (RET_QK_WIDTH, RET_DV))
    return log_gamma, dmat, toend, fstart, cdec


def _np01(a):
    return jnp.asarray(np.asarray(a, np.float32), BF16)


def kernel(x_prompt, x_sample, cache_sb_k, cache_sb_v, page_table, state_ssd_conv, state_ssd, state_ret,
           norm_mix, w_in, conv_w, conv_b, dt_bias, a_log, d_skip, ssd_norm_w, sb_q_norm, sb_k_norm,
           sb_bias, ret_norm_w, w_gate, b_gate, w_br_ssd, w_br_sb, w_br_ret, w_out, norm_mlp, w_up, w_down):
    b_p, seq, d_model = x_prompt.shape
    b_s, t_new, _ = x_sample.shape
    depth = w_in.shape[0]
    n_pages = page_table.shape[1]
    past = n_pages * PAGE_SIZE
    assert b_p == 1 and t_new == 1, "one prompt sequence and one new token per sample sequence"
    assert seq % SB_TQ == 0 and b_s % 8 == 0 and b_s <= LANES and n_pages % DEC_PAGES == 0

    ii = np.arange(CHUNK)
    tril = _np01(ii[:, None] >= ii[None, :])
    triu = _np01(ii[:, None] <= ii[None, :])
    neg_suffix = lambda m: _np01(-(np.arange(m)[:, None] >= np.arange(m)[None, :]).astype(np.float32))
    uneg_page, uneg_blk = neg_suffix(PAGE_SIZE), neg_suffix(SB_KEYS)
    tail_np = np.zeros((LANES, 2 * SB_KEYS), np.float32)
    tail_np[0:2, :SB_KEYS] = 1.0
    tail_np[2:4, SB_KEYS:] = 1.0
    tail = _np01(tail_np)
    hh = np.arange(SB_WIDTH) // SB_HEAD_DIM
    seg = _np01(hh[:, None] == hh[None, :])
    head_of = np.arange(HEAD_ROWS)[:, None] == hh[None, :]
    diag = jnp.asarray(head_of, F32)
    log_gamma, dmat, toend, fstart, cdec = _ret_tables()
    cos_p, sin_p = _rope_tables(jnp.arange(seq, dtype=jnp.int32))
    cos_s, sin_s = _rope_tables(past + jnp.arange(t_new, dtype=jnp.int32))
    gam = jnp.tile(jnp.exp(log_gamma), b_s).reshape(b_s * RET_HEADS, 1, 1)

    sizes = [SSD_WIDTH, SSD_CONV_DIM, SSD_HEADS, SB_WIDTH, SB_WIDTH, SB_WIDTH,
             RET_QK_WIDTH, RET_QK_WIDTH, RET_V_WIDTH, RET_V_WIDTH]
    offs = np.concatenate([[0], np.cumsum(sizes)])
    cols = {name: slice(int(offs[j]), int(offs[j + 1]))
            for j, name in enumerate(["z", "xbc", "dt", "sq", "sk", "sv", "rq", "rk", "rv", "rg"])}
    pt_flat = page_table.reshape(-1)
    page_view = lambda c: jnp.transpose(c, (0, 1, 3, 4, 2)).reshape(depth * c.shape[1], SB_WIDTH, PAGE_SIZE)
    ckt, cvt = page_view(cache_sb_k), page_view(cache_sb_v)

    xp = x_prompt.reshape(seq, d_model)
    xs = x_sample.reshape(b_s, d_model)
    outs = [[] for _ in range(10)]
    for l in range(depth):
        wl = w_in[l]
        w_main = jnp.concatenate([wl[:, cols[c]] for c in ("z", "xbc", "sq", "sv", "rq", "rk", "rv", "rg")],
                                 axis=1).astype(BF16)
        w_kvt = jnp.concatenate([wl[:, cols["sk"]], wl[:, cols["sv"]]], axis=1).T.astype(BF16)
        w_dtc = jnp.pad(wl[:, cols["dt"]], ((0, 0), (0, LANES - SSD_HEADS))).astype(BF16)
        w_dtt = w_dtc[:, :HEAD_ROWS].T
        nw = norm_mix[l][None, :]
        qn = jnp.tile(sb_q_norm[l], SB_HEADS)[None, :]
        kn_col = jnp.tile(sb_k_norm[l], SB_HEADS)[:, None]
        a_neg = -jnp.exp(a_log[l])
        pad_h = lambda v: jnp.pad(v, (0, LANES - SSD_HEADS))
        dtb_row, aneg_row = pad_h(dt_bias[l])[None, :], pad_h(a_neg)[None, :]
        dtb_col, aneg_col = pad_h(dt_bias[l])[:HEAD_ROWS, None], pad_h(a_neg)[:HEAD_ROWS, None]
        dskip = jnp.repeat(d_skip[l], SSD_HEAD_DIM)[None, :]
        snw, rnw = ssd_norm_w[l][None, :], ret_norm_w[l][None, :]
        cw, cb = conv_w[l], conv_b[l][None, :]
        wg, bg = w_gate[l].astype(BF16), b_gate[l][None, :]
        wa, wb, wc = w_br_ssd[l].astype(BF16), w_br_sb[l].astype(BF16), w_br_ret[l].astype(BF16)
        wo, wu, wd = w_out[l].astype(BF16), w_up[l].astype(BF16), w_down[l].astype(BF16)
        nm = norm_mlp[l][None, :]
        kv_out = lambda t, b, rows: jnp.transpose(t.reshape(SB_HEADS, SB_HEAD_DIM, b, rows), (2, 3, 0, 1))

        bias2 = sb_bias[l] * LOG2E
        b_hi = bias2.astype(BF16)
        b_lo = (bias2 - b_hi.astype(F32)).astype(BF16)
        qbias = jnp.pad(jnp.stack([b_hi, b_lo], axis=1).reshape(SB_HEADS // 2, 4),
                        ((0, 0), (0, LANES - 4))).reshape(1, SB_WIDTH)
        p = _in_proj(xp, nw, w_main, w_kvt, w_dtc, w_dtt, qn, kn_col, seg, qbias)
        o_ssd, st_ssd = _ssd_prompt(p, cw, cb, dtb_row, dtb_col, aneg_row, aneg_col, dskip, snw, tril, triu)
        o_sb = _sb_prompt(p["qp"], p["kblk"], p["vb"], tail, uneg_blk, SB_TQ)
        o_ret, st_ret = _ret_prompt(p, cos_p, sin_p, dmat, toend, fstart, cdec, rnw)
        xp = _mlp(_merge(xp, o_ssd, o_sb, o_ret, nw, wg, bg, wa, wb, wc, wo), nm, wu, wd)
        outs[0].append(kv_out(p["kt"], b_p, seq))
        outs[1].append(kv_out(p["vt"], b_p, seq))
        outs[4].append(p["xbc"][seq - (SSD_CONV - 1):].reshape(b_p, SSD_CONV - 1, SSD_CONV_DIM))
        outs[6].append(st_ssd.reshape(b_p, SSD_HEADS, SSD_HEAD_DIM, SSD_STATE))
        outs[8].append(st_ret.reshape(b_p, RET_HEADS, RET_DK, RET_DV))

        s = _in_proj(xs, nw, w_main, w_kvt, w_dtc, w_dtt, qn, kn_col, seg)
        cp = jnp.swapaxes(state_ssd_conv[l], 0, 1)
        xc, dt, da, rq, rk = _dec_prep(s["xbc"], cp, cw, cb, s["dtc"], dtb_row, aneg_row, s["rq"], s["rk"],
                                       cos_s, sin_s)
        x_ssd = xc[:, :SSD_WIDTH]
        per_head = lambda m: jnp.repeat(m.reshape(b_s, SSD_GROUPS, 1, SSD_STATE), SSD_HEADS // SSD_GROUPS,
                                        axis=1).reshape(b_s * SSD_HEADS, 1, SSD_STATE)
        bh = per_head(xc[:, SSD_WIDTH:SSD_WIDTH + SSD_GROUPS * SSD_STATE])
        ch = per_head(xc[:, SSD_WIDTH + SSD_GROUPS * SSD_STATE:])
        s_new, y_col, r_new, yr = _dec_state(
            state_ssd[l].reshape(b_s * SSD_HEADS, SSD_HEAD_DIM, SSD_STATE),
            x_ssd.reshape(b_s * SSD_HEADS, SSD_HEAD_DIM, 1),
            dt[:, :SSD_HEADS].reshape(b_s * SSD_HEADS, 1, 1), da[:, :SSD_HEADS].reshape(b_s * SSD_HEADS, 1, 1),
            bh, ch,
            state_ret[l].reshape(b_s * RET_HEADS, RET_DK, RET_DV),
            rk.reshape(b_s * RET_HEADS, RET_DK, 1), rq.reshape(b_s * RET_HEADS, RET_DK, 1),
            s["rv"].reshape(b_s * RET_HEADS, 1, RET_DV), gam, 8)
        o_ssd_s, o_ret_s = _dec_post(y_col.reshape(b_s, SSD_WIDTH), x_ssd, dskip, s["z"], snw,
                                     yr.reshape(b_s, RET_V_WIDTH), s["rg"], rnw)
        qbdt = jnp.where(head_of[None], s["qb"][:, None, :], jnp.zeros((), BF16))
        bias_col = pad_h(sb_bias[l])[:HEAD_ROWS, None]
        pad_b = lambda t: jnp.pad(t, ((0, 0), (0, LANES - b_s)))
        o_sb_s = _sb_decode(pt_flat, ckt, cvt, l, bias_col, qbdt, pad_b(s["kt"]), pad_b(s["vt"]), uneg_page, diag,
                            DEC_PAGES).reshape(b_s, SB_WIDTH)
        xs = _mlp(_merge(xs, o_ssd_s, o_sb_s, o_ret_s, nw, wg, bg, wa, wb, wc, wo), nm, wu, wd)
        outs[2].append(kv_out(s["kt"], b_s, t_new))
        outs[3].append(kv_out(s["vt"], b_s, t_new))
        outs[5].append(jnp.concatenate([state_ssd_conv[l][:, 1:], s["xbc"][:, None, :]], axis=1))
        outs[7].append(s_new.reshape(b_s, SSD_HEADS, SSD_HEAD_DIM, SSD_STATE))
        outs[9].append(r_new.reshape(b_s, RET_HEADS, RET_DK, RET_DV))

    stacked = [jnp.stack(o) for o in outs]
    return (xp.reshape(b_p, seq, d_model), xs.reshape(b_s, t_new, d_model), *stacked)
```

```python
import functools

import numpy as np
import jax
import jax.numpy as jnp
from jax import lax
from jax.experimental import pallas as pl
from jax.experimental.pallas import tpu as pltpu

F32 = jnp.float32
BF16 = jnp.bfloat16

SSD_HEAD_DIM = 64
SSD_HEADS = 8
SSD_GROUPS = 2
SSD_STATE = 128
SSD_WIDTH = SSD_HEADS * SSD_HEAD_DIM
SSD_CONV = 4
SSD_CONV_DIM = SSD_WIDTH + 2 * SSD_GROUPS * SSD_STATE
SB_HEADS = 8
SB_HEAD_DIM = 64
SB_WIDTH = SB_HEADS * SB_HEAD_DIM
RET_HEADS = 4
RET_DK = 64
RET_DV = 128
RET_QK_WIDTH = RET_HEADS * RET_DK
RET_V_WIDTH = RET_HEADS * RET_DV
ROPE_BASE = 10000.0
N_BRANCH = 3
EPS = 1e-6
LOG2E = 1.4426950408889634
CHUNK = 128
PAGE_SIZE = 128
SB_KEYS = 256
SB_TQ = 512
SB_ROWS = 256
SB_MAX_LOGIT2 = 126.0
DEC_PAGES = 16

LANES = 128
VMEM_LIMIT = 56 * 1024 * 1024

_NT = (((1,), (1,)), ((), ()))
_TN = (((0,), (0,)), ((), ()))


def _cparams(*sem):
    return pltpu.CompilerParams(dimension_semantics=sem, vmem_limit_bytes=VMEM_LIMIT)


def _split_bf16(a, terms):
    parts, rem = [], a
    for _ in range(terms):
        p = rem.astype(BF16)
        parts.append(p)
        rem = rem - p.astype(F32)
    return parts


def _dot_rhs01(a, m01, terms):
    out = None
    for p in _split_bf16(a, terms):
        d = jnp.dot(p, m01, preferred_element_type=F32)
        out = d if out is None else out + d
    return out


def _dot_lhs01(m01, a, terms):
    out = None
    for p in _split_bf16(a, terms):
        d = jnp.dot(m01, p, preferred_element_type=F32)
        out = d if out is None else out + d
    return out


def _softplus(x):
    return jnp.maximum(x, 0.0) + jnp.log1p(jnp.exp(-jnp.abs(x)))


def _silu(x):
    return x * (1.0 / (1.0 + jnp.exp(-x)))


def _rms(x, w):
    return x * lax.rsqrt(jnp.mean(x * x, axis=-1, keepdims=True) + EPS) * w


def _full(shape):
    nd = len(shape)
    return pl.BlockSpec(shape, lambda *_: (0,) * nd)


_IN_COLS = dict(z=(0, 512), xbc=(512, 1536), sq=(1536, 2048), sv=(2048, 2560),
                rq=(2560, 2816), rk=(2816, 3072), rv=(3072, 3584), rg=(3584, 4096))


def _in_proj_kernel(*refs, prompt):
    (x_ref, nw_ref, w_ref, wkvt_ref, wdtc_ref, wdtt_ref, qn_ref, kn_ref, seg_ref) = refs[:9]
    refs = refs[9:]
    if prompt:
        qbias_ref, refs = refs[0], refs[1:]
    (z_ref, xbc_ref, dtc_ref, dtt_ref, qb_ref, kt_ref, vt_ref, rq_ref, rk_ref, rv_ref, rg_ref) = refs[:11]
    hb = _rms(x_ref[...], nw_ref[...]).astype(BF16)
    tm = hb.shape[0]

    def proj(name):
        lo, hi = _IN_COLS[name]
        return jnp.dot(hb, w_ref[:, lo:hi], preferred_element_type=F32)

    z_ref[...] = proj("z")
    xbc_ref[...] = proj("xbc")
    dtc_ref[...] = jnp.dot(hb, wdtc_ref[...], preferred_element_type=F32)
    dtt_ref[...] = lax.dot_general(wdtt_ref[...], hb, _NT, preferred_element_type=F32)
    sq = proj("sq")
    ms = _dot_rhs01(sq * sq, seg_ref[...], 2) * (1.0 / SB_HEAD_DIM)
    qf = sq * lax.rsqrt(ms + EPS) * qn_ref[...] * (SB_HEAD_DIM ** -0.5)
    qb_ref[...] = qf.astype(BF16)
    kvt = lax.dot_general(wkvt_ref[...], hb, _NT, preferred_element_type=F32)
    k3 = kvt[:SB_WIDTH].reshape(SB_HEADS, SB_HEAD_DIM, tm)
    k3 = k3 * lax.rsqrt(jnp.mean(k3 * k3, axis=1, keepdims=True) + EPS)
    kt = k3.reshape(SB_WIDTH, tm) * kn_ref[...]
    kt_ref[...] = kt
    vt_ref[...] = kvt[SB_WIDTH:]
    rq_ref[...] = proj("rq")
    rk_ref[...] = proj("rk")
    rv_ref[...] = proj("rv")
    rg_ref[...] = proj("rg")
    if prompt:
        qp_ref, kblk_ref, vb_ref = refs[11:]
        ktb = kt.astype(BF16)
        qb = (qf * LOG2E).astype(BF16)
        for p in range(SB_HEADS // 2):
            qp_ref[:, 2 * p * LANES:(2 * p + 1) * LANES] = qb[:, p * LANES:(p + 1) * LANES]
            qp_ref[:, (2 * p + 1) * LANES:(2 * p + 2) * LANES] = jnp.broadcast_to(
                qbias_ref[:, p * LANES:(p + 1) * LANES], (tm, LANES))
            for j in range(tm // SB_KEYS):
                kblk_ref[p, j] = ktb[p * LANES:(p + 1) * LANES, j * SB_KEYS:(j + 1) * SB_KEYS]
        vb_ref[...] = proj("sv").astype(BF16)


def _in_proj(x, nw, w_main, w_kvt, w_dtc, w_dtt, qn, kn_col, seg, qbias=None):
    n, d = x.shape
    tm = min(n, 256)
    prompt = qbias is not None
    row = lambda c: pl.BlockSpec((tm, c), lambda i: (i, 0))
    colmajor = lambda r: pl.BlockSpec((r, tm), lambda i: (0, i))
    outs = [("z", (n, 512), F32, row(512)), ("xbc", (n, 1024), F32, row(1024)),
            ("dtc", (n, LANES), F32, row(LANES)), ("dtt", (16, n), F32, colmajor(16)),
            ("qb", (n, 512), BF16, row(512)), ("kt", (SB_WIDTH, n), F32, colmajor(SB_WIDTH)),
            ("vt", (SB_WIDTH, n), F32, colmajor(SB_WIDTH)),
            ("rq", (n, 256), F32, row(256)), ("rk", (n, 256), F32, row(256)),
            ("rv", (n, 512), F32, row(512)), ("rg", (n, 512), F32, row(512))]
    args = [x, nw, w_main, w_kvt, w_dtc, w_dtt, qn, kn_col, seg]
    if prompt:
        npair, per = SB_HEADS // 2, tm // SB_KEYS
        outs += [("qp", (n, 2 * SB_WIDTH), BF16, row(2 * SB_WIDTH)),
                 ("kblk", (npair, n // SB_KEYS, LANES, SB_KEYS), BF16,
                  pl.BlockSpec((npair, per, LANES, SB_KEYS), lambda i: (0, i, 0, 0))),
                 ("vb", (n, 512), BF16, row(512))]
        args.append(qbias)
    res = pl.pallas_call(
        functools.partial(_in_proj_kernel, prompt=prompt),
        grid=(n // tm,),
        in_specs=[row(d)] + [_full(a.shape) for a in args[1:]],
        out_specs=[o[3] for o in outs],
        out_shape=[jax.ShapeDtypeStruct(o[1], o[2]) for o in outs],
        compiler_params=_cparams("arbitrary"),
    )(*args)
    return dict(zip([o[0] for o in outs], res))


def _ssd_prompt_kernel(z_ref, xbc_ref, dtc_ref, dtt_ref, cw_ref, cb_ref, dtb_row_ref, dtb_col_ref,
                       aneg_row_ref, aneg_col_ref, dskip_ref, nw_ref, tril_ref, triu_ref,
                       o_ref, st_out_ref, xbuf, st):
    i = pl.program_id(0)
    c = CHUNK

    @pl.when(i == 0)
    def _():
        xbuf[0:8, :] = jnp.zeros((8, SSD_CONV_DIM), F32)
        st[...] = jnp.zeros_like(st)

    xbuf[8:8 + c, :] = xbc_ref[...]
    acc = cb_ref[...] + cw_ref[SSD_CONV - 1:SSD_CONV, :] * xbuf[8:8 + c, :]
    for k in range(SSD_CONV - 1):
        off = 8 - (SSD_CONV - 1) + k
        acc = acc + cw_ref[k:k + 1, :] * xbuf[off:off + c, :]
    xbuf[0:8, :] = xbuf[c:c + 8, :]
    xc = _silu(acc)
    xs = xc[:, :SSD_WIDTH]

    dt_c = _softplus(dtc_ref[...] + dtb_row_ref[...])
    dt_t = _softplus(dtt_ref[...] + dtb_col_ref[...])
    cum_c = _dot_lhs01(tril_ref[...], dt_c * aneg_row_ref[...], 3)
    cum_t = _dot_rhs01(dt_t * aneg_col_ref[...], triu_ref[...], 3)

    row = lax.broadcasted_iota(jnp.int32, (c, c), 0)
    col = lax.broadcasted_iota(jnp.int32, (c, c), 1)
    causal = row >= col
    first_half = lax.broadcasted_iota(jnp.int32, (c, LANES), 1) < SSD_HEAD_DIM

    ys = []
    for pair in range(SSD_HEADS // 2):
        g = pair // 2
        h0, h1 = 2 * pair, 2 * pair + 1
        bm = xc[:, SSD_WIDTH + g * SSD_STATE:SSD_WIDTH + (g + 1) * SSD_STATE].astype(BF16)
        cm = xc[:, SSD_WIDTH + (SSD_GROUPS + g) * SSD_STATE:
                SSD_WIDTH + (SSD_GROUPS + g + 1) * SSD_STATE].astype(BF16)
        cb = lax.dot_general(cm, bm, _NT, preferred_element_type=F32)
        xs_p = xs[:, pair * LANES:(pair + 1) * LANES]
        cc0, cc1 = cum_c[:, h0:h0 + 1], cum_c[:, h1:h1 + 1]
        dt_p = jnp.where(first_half, dt_c[:, h0:h0 + 1], dt_c[:, h1:h1 + 1])
        cc_p = jnp.where(first_half, cc0, cc1)
        last_p = jnp.where(first_half[0:1, :], cum_c[c - 1:c, h0:h0 + 1], cum_c[c - 1:c, h1:h1 + 1])
        xdt = xs_p * dt_p
        xdt_b = xdt.astype(BF16)
        yd = []
        for h, cc in ((h0, cc0), (h1, cc1)):
            seg = cc - cum_t[h:h + 1, :]
            decay = jnp.exp(jnp.where(causal, seg, -jnp.inf))
            yd.append(jnp.dot((cb * decay).astype(BF16), xdt_b, preferred_element_type=F32))
        y_diag = jnp.where(first_half, yd[0], yd[1])
        ent = st[pair * LANES:(pair + 1) * LANES, :]
        y_off = lax.dot_general(cm, ent.astype(BF16), _NT, preferred_element_type=F32) * jnp.exp(cc_p)
        xw = (xdt * jnp.exp(last_p - cc_p)).astype(BF16)
        contrib = lax.dot_general(xw, bm, _TN, preferred_element_type=F32)
        dec0 = jnp.exp(cum_c[c - 1:c, h0:h0 + 1])
        dec1 = jnp.exp(cum_c[c - 1:c, h1:h1 + 1])
        upper = lax.broadcasted_iota(jnp.int32, (LANES, SSD_STATE), 0) < SSD_HEAD_DIM
        st[pair * LANES:(pair + 1) * LANES, :] = ent * jnp.where(upper, dec0, dec1) + contrib
        ys.append(y_diag + y_off + dskip_ref[:, pair * LANES:(pair + 1) * LANES] * xs_p)

    y = jnp.concatenate(ys, axis=-1)
    u = y * _silu(z_ref[...])
    gw = SSD_WIDTH // SSD_GROUPS
    outs = []
    for g in range(SSD_GROUPS):
        ug = u[:, g * gw:(g + 1) * gw]
        outs.append(ug * lax.rsqrt(jnp.mean(ug * ug, axis=-1, keepdims=True) + EPS))
    o_ref[...] = jnp.concatenate(outs, axis=-1) * nw_ref[...]
    st_out_ref[...] = st[...]


def _ssd_prompt(p, cw, cb, dtb_row, dtb_col, aneg_row, aneg_col, dskip, nw, tril, triu):
    n = p["z"].shape[0]
    c = CHUNK
    row = lambda w: pl.BlockSpec((c, w), lambda i: (i, 0))
    consts = [cw, cb, dtb_row, dtb_col, aneg_row, aneg_col, dskip, nw, tril, triu]
    o, st = pl.pallas_call(
        _ssd_prompt_kernel,
        grid=(n // c,),
        in_specs=[row(SSD_WIDTH), row(SSD_CONV_DIM), row(LANES), pl.BlockSpec((16, c), lambda i: (0, i))]
                 + [_full(a.shape) for a in consts],
        out_specs=[row(SSD_WIDTH), _full((SSD_WIDTH, SSD_STATE))],
        out_shape=[jax.ShapeDtypeStruct((n, SSD_WIDTH), F32),
                   jax.ShapeDtypeStruct((SSD_WIDTH, SSD_STATE), F32)],
        scratch_shapes=[pltpu.VMEM((c + 8, SSD_CONV_DIM), F32), pltpu.VMEM((SSD_WIDTH, SSD_STATE), F32)],
        compiler_params=_cparams("arbitrary"),
    )(p["z"], p["xbc"], p["dtc"], p["dtt"], *consts)
    return o, st


def _rotate(x, cosf, sinf):
    w = x.shape[-1]
    half = RET_DK // 2
    ahead = pltpu.roll(x, w - half, 1)
    behind = pltpu.roll(x, half, 1)
    lane = lax.broadcasted_iota(jnp.int32, x.shape, 1) % RET_DK
    return x * cosf + jnp.where(lane < half, ahead, behind) * sinf


def _ret_prompt_kernel(rq_ref, rk_ref, rv_ref, rg_ref, cos_ref, sin_ref, dmat_ref, toend_ref, fstart_ref,
                       cdec_ref, nw_ref, o_ref, st_out_ref, st):
    i = pl.program_id(0)
    c = CHUNK

    @pl.when(i == 0)
    def _():
        st[...] = jnp.zeros_like(st)

    cosf, sinf = cos_ref[...], sin_ref[...]
    q = _rotate(rq_ref[...], cosf, sinf)
    k = _rotate(rk_ref[...], cosf, sinf) * (RET_DK ** -0.5)
    kw = (k * toend_ref[...]).astype(BF16)
    kb = k.astype(BF16)
    first_half = lax.broadcasted_iota(jnp.int32, (c, LANES), 1) < RET_DK
    ys = []
    for h in range(RET_HEADS):
        pair, odd = h // 2, h % 2
        sl = slice(pair * LANES, (pair + 1) * LANES)
        qm = jnp.where(first_half != bool(odd), q[:, sl], 0.0).astype(BF16)
        v = rv_ref[:, h * RET_DV:(h + 1) * RET_DV].astype(BF16)
        scores = lax.dot_general(qm, kb[:, sl], _NT, preferred_element_type=F32) * dmat_ref[h]
        y = jnp.dot(scores.astype(BF16), v, preferred_element_type=F32)
        ent = st[pair * LANES:(pair + 1) * LANES, :]
        y = y + jnp.dot(qm, ent.astype(BF16), preferred_element_type=F32) * fstart_ref[:, h * RET_DV:(h + 1) * RET_DV]
        contrib = lax.dot_general(kw[:, sl], v, _TN, preferred_element_type=F32)
        r0 = h * RET_DK
        st[r0:r0 + RET_DK, :] = (st[r0:r0 + RET_DK, :] * cdec_ref[r0:r0 + RET_DK, :]
                                 + contrib[odd * RET_DK:(odd + 1) * RET_DK, :])
        yc = y - jnp.mean(y, axis=-1, keepdims=True)
        ys.append(yc * lax.rsqrt(jnp.mean(yc * yc, axis=-1, keepdims=True) + EPS))
    y = jnp.concatenate(ys, axis=-1) * nw_ref[...]
    o_ref[...] = _silu(rg_ref[...]) * y
    st_out_ref[...] = st[...]


def _ret_prompt(p, cosf, sinf, dmat, toend, fstart, cdec, nw):
    n = p["rq"].shape[0]
    c = CHUNK
    row = lambda w: pl.BlockSpec((c, w), lambda i: (i, 0))
    consts = [dmat, toend, fstart, cdec, nw]
    o, st = pl.pallas_call(
        _ret_prompt_kernel,
        grid=(n // c,),
        in_specs=[row(RET_QK_WIDTH), row(RET_QK_WIDTH), row(RET_V_WIDTH), row(RET_V_WIDTH),
                  row(RET_QK_WIDTH), row(RET_QK_WIDTH)] + [_full(a.shape) for a in consts],
        out_specs=[row(RET_V_WIDTH), _full((RET_QK_WIDTH, RET_DV))],
        out_shape=[jax.ShapeDtypeStruct((n, RET_V_WIDTH), F32),
                   jax.ShapeDtypeStruct((RET_QK_WIDTH, RET_DV), F32)],
        scratch_shapes=[pltpu.VMEM((RET_QK_WIDTH, RET_DV), F32)],
        compiler_params=_cparams("arbitrary"),
    )(p["rq"], p["rk"], p["rv"], p["rg"], cosf, sinf, *consts)
    return o, st


def _neg_abs(x):
    sign = jnp.uint32(0x80000000)
    return lax.bitcast_convert_type(lax.bitcast_convert_type(x, jnp.uint32) | sign, F32)


def _sb_weights(z2, uneg, carries, key_off):
    tq, keys = z2.shape[0], z2.shape[1] // 2
    causal = None
    if key_off is not None:
        row = lax.broadcasted_iota(jnp.int32, (tq, keys), 0)
        col = lax.broadcasted_iota(jnp.int32, (tq, keys), 1)
        causal = col + key_off < row
    a_parts, new_carries = [], []
    for h in range(2):
        z = z2[:, h * keys:(h + 1) * keys]
        sp = jnp.log(1.0 + jnp.exp2(jnp.minimum(z, SB_MAX_LOGIT2))) * LOG2E
        if causal is not None:
            sp = jnp.where(causal, sp, 0.0)
        s = jnp.dot(sp.astype(BF16), uneg, preferred_element_type=F32)
        c = carries[h]
        a = jnp.exp2(z + s + jnp.concatenate([c] * (keys // LANES), axis=1))
        if causal is not None:
            a = jnp.where(causal, a, 0.0)
        a_parts.append(a.astype(BF16))
        new_carries.append(c + jnp.broadcast_to(s[:, 0:1], c.shape))
    return jnp.concatenate(a_parts, axis=1), new_carries


def _sb_prompt_kernel(dead_ref, q_ref, k_ref, v_ref, tail_ref, uneg_ref, o_ref, zbuf, abuf, cbuf, accbuf):
    pair, i = pl.program_id(0), pl.program_id(1)
    tq, keys = q_ref.shape[0], k_ref.shape[2]
    assert tq == 2 * keys
    qp, tail, uneg = q_ref[...], tail_ref[...], uneg_ref[...]
    zk = jnp.zeros((SB_HEAD_DIM, keys), BF16)

    def key_weights(blk):
        k2 = k_ref[blk]
        return jnp.concatenate([jnp.concatenate([k2[:SB_HEAD_DIM], zk], axis=1),
                                jnp.concatenate([zk, k2[SB_HEAD_DIM:]], axis=1), tail], axis=0)

    def value_weights(blk):
        v = v_ref[pl.ds(pl.multiple_of(blk * keys, keys), keys), :]
        lane_lo = lax.broadcasted_iota(jnp.int32, v.shape, 1) < SB_HEAD_DIM
        zero = jnp.zeros_like(v)
        return jnp.concatenate([jnp.where(lane_lo, v, zero), jnp.where(lane_lo, zero, v)], axis=0)

    def scores(blk):
        return jnp.dot(qp, key_weights(blk), preferred_element_type=F32)

    def weighted_values(a, vbd):
        return jnp.dot(a, vbd, preferred_element_type=F32)

    zero = jnp.zeros((tq, LANES), F32)
    hi, lo = 2 * i + 1, 2 * i
    a_hi, carries = _sb_weights(scores(hi), uneg, [zero, zero], keys)
    accbuf[...] = weighted_values(a_hi, value_weights(hi))
    a_lo, carries = _sb_weights(scores(lo), uneg, carries, 0)
    abuf[...] = a_lo
    cbuf[0], cbuf[1] = carries
    zbuf[...] = scores(jnp.maximum(lo - 1, 0))

    dead0, dead1 = dead_ref[2 * pair], dead_ref[2 * pair + 1]

    def step(state):
        t, _, _ = state
        done = jnp.logical_and(jnp.max(cbuf[0]) < dead0, jnp.max(cbuf[1]) < dead1)
        hi = 2 * (i - t) - 1
        lo = hi - 1
        kb_lo, kb_next = key_weights(lo), key_weights(jnp.maximum(lo - 1, 0))
        vb_prev, vb_hi = value_weights(lo + 2), value_weights(hi)
        for r in range(tq // SB_ROWS):
            rows = slice(r * SB_ROWS, (r + 1) * SB_ROWS)
            z_lo = jnp.dot(qp[rows], kb_lo, preferred_element_type=F32)
            a_hi, carries = _sb_weights(zbuf[rows, :], uneg, [cbuf[0, rows, :], cbuf[1, rows, :]], None)
            acc = weighted_values(abuf[rows, :], vb_prev) + weighted_values(a_hi, vb_hi)
            a_lo, carries = _sb_weights(z_lo, uneg, carries, None)
            abuf[rows, :] = a_lo
            cbuf[0, rows, :], cbuf[1, rows, :] = carries
            zbuf[rows, :] = jnp.dot(qp[rows], kb_next, preferred_element_type=F32)
            accbuf[rows, :] += acc
        return t + 1, done.astype(jnp.int32), lo

    _, _, last = lax.while_loop(lambda s: jnp.logical_and(s[0] < i, s[1] == 0), step,
                                (jnp.int32(0), jnp.int32(0), lo))
    o_ref[...] = accbuf[...] + weighted_values(abuf[...], value_weights(last))


def _sb_prompt(dead, qp, kblk, vb, tail, uneg, tq):
    n = qp.shape[0]
    npair = SB_HEADS // 2
    return pl.pallas_call(
        _sb_prompt_kernel,
        grid_spec=pltpu.PrefetchScalarGridSpec(
            num_scalar_prefetch=1,
            grid=(npair, n // tq),
            in_specs=[pl.BlockSpec((tq, 2 * LANES), lambda p, i, d: (i, p)),
                      pl.BlockSpec((None,) + kblk.shape[1:], lambda p, i, d: (p, 0, 0, 0)),
                      pl.BlockSpec((n, LANES), lambda p, i, d: (0, p)),
                      pl.BlockSpec(tail.shape, lambda p, i, d: (0, 0)),
                      pl.BlockSpec(uneg.shape, lambda p, i, d: (0, 0))],
            out_specs=pl.BlockSpec((tq, LANES), lambda p, i, d: (i, p)),
            scratch_shapes=[pltpu.VMEM((tq, 2 * SB_KEYS), F32), pltpu.VMEM((tq, 2 * SB_KEYS), BF16),
                            pltpu.VMEM((2, tq, LANES), F32), pltpu.VMEM((tq, LANES), F32)],
        ),
        out_shape=jax.ShapeDtypeStruct((n, SB_WIDTH), F32),
        compiler_params=_cparams("arbitrary", "arbitrary"),
    )(dead, qp, kblk, vb, tail, uneg)


def _merge_kernel(x_ref, a_ref, b_ref, c_ref, nw_ref, wg_ref, bg_ref, wa_ref, wb_ref, wc_ref, wo_ref, o_ref):
    x = x_ref[...]
    d = x.shape[-1]
    hb = _rms(x, nw_ref[...]).astype(BF16)
    merged = None
    for j, (br_ref, w_ref) in enumerate(((a_ref, wa_ref), (b_ref, wb_ref), (c_ref, wc_ref))):
        gate = jax.nn.sigmoid(jnp.dot(hb, wg_ref[:, j * d:(j + 1) * d], preferred_element_type=F32)
                              + bg_ref[:, j * d:(j + 1) * d])
        t = gate * jnp.dot(br_ref[...].astype(BF16), w_ref[...], preferred_element_type=F32)
        merged = t if merged is None else merged + t
    o_ref[...] = x + jnp.dot(merged.astype(BF16), wo_ref[...], preferred_element_type=F32)


def _merge(x, o_ssd, o_sb, o_ret, nw, wg, bg, wa, wb, wc, wo):
    n, d = x.shape
    tm = min(n, 256)
    row = lambda w: pl.BlockSpec((tm, w), lambda i: (i, 0))
    consts = [nw, wg, bg, wa, wb, wc, wo]
    return pl.pallas_call(
        _merge_kernel,
        grid=(n // tm,),
        in_specs=[row(d), row(o_ssd.shape[1]), row(o_sb.shape[1]), row(o_ret.shape[1])]
                 + [_full(a.shape) for a in consts],
        out_specs=row(d),
        out_shape=jax.ShapeDtypeStruct((n, d), F32),
        compiler_params=_cparams("arbitrary"),
    )(x, o_ssd, o_sb, o_ret, *consts)


def _mlp_kernel(x_ref, nw_ref, wu_ref, wd_ref, o_ref):
    x = x_ref[...]
    hb = _rms(x, nw_ref[...]).astype(BF16)
    up = jnp.maximum(jnp.dot(hb, wu_ref[...], preferred_element_type=F32), 0.0)
    o_ref[...] = x + jnp.dot((up * up).astype(BF16), wd_ref[...], preferred_element_type=F32)


def _mlp(x, nw, wu, wd):
    n, d = x.shape
    tm = min(n, 256)
    row = pl.BlockSpec((tm, d), lambda i: (i, 0))
    return pl.pallas_call(
        _mlp_kernel,
        grid=(n // tm,),
        in_specs=[row, _full(nw.shape), _full(wu.shape), _full(wd.shape)],
        out_specs=row,
        out_shape=jax.ShapeDtypeStruct((n, d), F32),
        compiler_params=_cparams("arbitrary"),
    )(x, nw, wu, wd)


def _dec_prep_kernel(xbc_ref, cp_ref, cw_ref, cb_ref, dtc_ref, dtb_ref, aneg_ref, rq_ref, rk_ref,
                     cos_ref, sin_ref, xc_ref, dt_ref, da_ref, q_ref, k_ref):
    acc = cb_ref[...] + cw_ref[SSD_CONV - 1:SSD_CONV, :] * xbc_ref[...]
    for k in range(SSD_CONV - 1):
        acc = acc + cw_ref[k:k + 1, :] * cp_ref[k]
    xc_ref[...] = _silu(acc)
    dt = _softplus(dtc_ref[...] + dtb_ref[...])
    dt_ref[...] = dt
    da_ref[...] = jnp.exp(dt * aneg_ref[...])
    q_ref[...] = _rotate(rq_ref[...], cos_ref[...], sin_ref[...])
    k_ref[...] = _rotate(rk_ref[...], cos_ref[...], sin_ref[...]) * (RET_DK ** -0.5)


def _dec_prep(xbc, cp, cw, cb, dtc, dtb, aneg, rq, rk, cosr, sinr):
    b = xbc.shape[0]
    args = [xbc, cp, cw, cb, dtc, dtb, aneg, rq, rk, cosr, sinr]
    return pl.pallas_call(
        _dec_prep_kernel,
        grid=(1,),
        in_specs=[_full(a.shape) for a in args],
        out_specs=[_full((b, SSD_CONV_DIM)), _full((b, LANES)), _full((b, LANES)),
                   _full((b, RET_QK_WIDTH)), _full((b, RET_QK_WIDTH))],
        out_shape=[jax.ShapeDtypeStruct((b, SSD_CONV_DIM), F32), jax.ShapeDtypeStruct((b, LANES), F32),
                   jax.ShapeDtypeStruct((b, LANES), F32), jax.ShapeDtypeStruct((b, RET_QK_WIDTH), F32),
                   jax.ShapeDtypeStruct((b, RET_QK_WIDTH), F32)],
        compiler_params=_cparams("arbitrary"),
    )(*args)


def _dec_state_kernel(s_ref, xs_ref, dt_ref, da_ref, bh_ref, ch_ref, r_ref, kc_ref, qc_ref, vr_ref, gam_ref,
                      s_out_ref, y_ref, r_out_ref, yr_ref):
    s_new = s_ref[...] * da_ref[...] + (xs_ref[...] * dt_ref[...]) * bh_ref[...]
    s_out_ref[...] = s_new
    y_ref[...] = jnp.sum(s_new * ch_ref[...], axis=-1, keepdims=True)
    r_new = r_ref[...] * gam_ref[...] + kc_ref[...] * vr_ref[...]
    r_out_ref[...] = r_new
    yr_ref[...] = jnp.sum(qc_ref[...] * r_new, axis=-2, keepdims=True)


def _dec_state(s, xs_col, dt_col, da_col, bh, ch, r, k_col, q_col, v_row, gam, nb):
    b = s.shape[0] // SSD_HEADS
    hs, hr = nb * SSD_HEADS, nb * RET_HEADS
    blk = lambda rows, a, c: pl.BlockSpec((rows, a, c), lambda i: (i, 0, 0))
    return pl.pallas_call(
        _dec_state_kernel,
        grid=(b // nb,),
        in_specs=[blk(hs, SSD_HEAD_DIM, SSD_STATE), blk(hs, SSD_HEAD_DIM, 1), blk(hs, 1, 1), blk(hs, 1, 1),
                  blk(hs, 1, SSD_STATE), blk(hs, 1, SSD_STATE),
                  blk(hr, RET_DK, RET_DV), blk(hr, RET_DK, 1), blk(hr, RET_DK, 1), blk(hr, 1, RET_DV),
                  blk(hr, 1, 1)],
        out_specs=[blk(hs, SSD_HEAD_DIM, SSD_STATE), blk(hs, SSD_HEAD_DIM, 1),
                   blk(hr, RET_DK, RET_DV), blk(hr, 1, RET_DV)],
        out_shape=[jax.ShapeDtypeStruct(s.shape, F32), jax.ShapeDtypeStruct(xs_col.shape, F32),
                   jax.ShapeDtypeStruct(r.shape, F32), jax.ShapeDtypeStruct(v_row.shape, F32)],
        compiler_params=_cparams("arbitrary"),
    )(s, xs_col, dt_col, da_col, bh, ch, r, k_col, q_col, v_row, gam)


def _dec_post_kernel(y_ref, xs_ref, dskip_ref, z_ref, snw_ref, yr_ref, rg_ref, rnw_ref, o_ssd_ref, o_ret_ref):
    u = (y_ref[...] + dskip_ref[...] * xs_ref[...]) * _silu(z_ref[...])
    gw = SSD_WIDTH // SSD_GROUPS
    outs = []
    for g in range(SSD_GROUPS):
        ug = u[:, g * gw:(g + 1) * gw]
        outs.append(ug * lax.rsqrt(jnp.mean(ug * ug, axis=-1, keepdims=True) + EPS))
    o_ssd_ref[...] = jnp.concatenate(outs, axis=-1) * snw_ref[...]
    ys = []
    for h in range(RET_HEADS):
        y = yr_ref[:, h * RET_DV:(h + 1) * RET_DV]
        yc = y - jnp.mean(y, axis=-1, keepdims=True)
        ys.append(yc * lax.rsqrt(jnp.mean(yc * yc, axis=-1, keepdims=True) + EPS))
    o_ret_ref[...] = _silu(rg_ref[...]) * (jnp.concatenate(ys, axis=-1) * rnw_ref[...])


def _dec_post(y, xs, dskip, z, snw, yr, rg, rnw):
    b = y.shape[0]
    args = [y, xs, dskip, z, snw, yr, rg, rnw]
    return pl.pallas_call(
        _dec_post_kernel,
        grid=(1,),
        in_specs=[_full(a.shape) for a in args],
        out_specs=[_full((b, SSD_WIDTH)), _full((b, RET_V_WIDTH))],
        out_shape=[jax.ShapeDtypeStruct((b, SSD_WIDTH), F32), jax.ShapeDtypeStruct((b, RET_V_WIDTH), F32)],
        compiler_params=_cparams("arbitrary"),
    )(*args)


HEAD_ROWS = 16


def _sb_decode_kernel(pt_ref, bias_ref, qbdt_ref, knewt_ref, vnewt_ref, uo_ref, diag_ref, *rest, pages, past):
    k_refs, v_refs = rest[:pages], rest[pages:2 * pages]
    o_ref, acc, carry = rest[2 * pages], rest[2 * pages + 1], rest[2 * pages + 2]
    bi, c = pl.program_id(0), pl.program_id(1)
    bias = bias_ref[...]
    qbdt = qbdt_ref[0]
    uneg = uo_ref[...]

    def sums(kt, visible):
        z = jnp.dot(qbdt, kt.astype(BF16), preferred_element_type=F32) + bias
        sp = jnp.maximum(z, 0.0) + jnp.log(1.0 + jnp.exp(_neg_abs(z)))
        if visible is not None:
            sp = jnp.where(visible, sp, 0.0)
        return z, _dot_rhs01(sp, uneg, 2)

    def weigh(z, s, carry_in, vt, visible):
        a = jnp.exp(z + s + carry_in)
        if visible is not None:
            a = jnp.where(visible, a, 0.0)
        out = lax.dot_general(a.astype(BF16), vt.astype(BF16), _NT, preferred_element_type=F32)
        return out, carry_in + jnp.broadcast_to(s[:, 0:1], carry_in.shape)

    @pl.when(c == 0)
    def _():
        lane = lax.broadcasted_iota(jnp.int32, (HEAD_ROWS, LANES), 1)
        key_pos = jnp.full((HEAD_ROWS, LANES), past, jnp.int32)
        visible = (lane == bi) & (key_pos < past)
        z, s = sums(knewt_ref[...], visible)
        out, cnew = weigh(z, s, jnp.zeros((HEAD_ROWS, LANES), F32), vnewt_ref[...], visible)
        acc[...] = out
        carry[...] = cnew

    zs = [sums(k_ref[...], None) for k_ref in k_refs]
    cur = carry[...]
    total = acc[...]
    for (z, s), v_ref in zip(zs, v_refs):
        out, cur = weigh(z, s, cur, v_ref[...], None)
        total = total + out
    acc[...] = total
    carry[...] = cur

    @pl.when(c == pl.num_programs(1) - 1)
    def _():
        o_ref[0] = jnp.sum(acc[...] * diag_ref[...], axis=0, keepdims=True)


def _sb_decode(page_table, cache_kt, cache_vt, layer, bias_col, qbdt, knewt, vnewt, uo, diag, pages):
    b = qbdt.shape[0]
    n_pages = page_table.shape[0] // b
    n_phys = cache_kt.shape[0] // 2
    nc = n_pages // pages
    base = layer * n_phys

    def page_spec(j):
        def imap(bi, c, pt):
            logical = (nc - 1 - c) * pages + (pages - 1 - j)
            return (base + pt[bi * n_pages + logical], 0, 0)
        return pl.BlockSpec((None, SB_WIDTH, PAGE_SIZE), imap)

    const = lambda a: pl.BlockSpec(a.shape, lambda bi, c, pt: (0,) * a.ndim)
    return pl.pallas_call(
        functools.partial(_sb_decode_kernel, pages=pages, past=n_pages * PAGE_SIZE),
        grid_spec=pltpu.PrefetchScalarGridSpec(
            num_scalar_prefetch=1,
            grid=(b, nc),
            in_specs=[const(bias_col), pl.BlockSpec((1,) + qbdt.shape[1:], lambda bi, c, pt: (bi, 0, 0)),
                      const(knewt), const(vnewt), const(uo), const(diag)]
                     + [page_spec(j) for j in range(pages)] * 2,
            out_specs=pl.BlockSpec((1, 1, SB_WIDTH), lambda bi, c, pt: (bi, 0, 0)),
            scratch_shapes=[pltpu.VMEM((HEAD_ROWS, SB_WIDTH), F32), pltpu.VMEM((HEAD_ROWS, LANES), F32)],
        ),
        out_shape=jax.ShapeDtypeStruct((b, 1, SB_WIDTH), F32),
        compiler_params=_cparams("arbitrary", "arbitrary"),
    )(page_table, bias_col, qbdt, knewt, vnewt, uo, diag, *([cache_kt] * pages), *([cache_vt] * pages))


def _rope_tables(pos):
    half = RET_DK // 2
    inv = ROPE_BASE ** (-jnp.arange(half, dtype=F32) / half)
    ang = pos.astype(F32)[:, None] * inv[None, :]
    cos, sin = jnp.cos(ang), jnp.sin(ang)
    cosf = jnp.tile(jnp.concatenate([cos, cos], axis=-1), (1, RET_HEADS))
    sinf = jnp.tile(jnp.concatenate([-sin, sin], axis=-1), (1, RET_HEADS))
    return cosf, sinf


def _ret_tables():
    log_gamma = jnp.log1p(-jnp.exp2(-5.0 - jnp.arange(RET_HEADS, dtype=F32)))
    idx = jnp.arange(CHUNK, dtype=F32)
    rel = idx[:, None] - idx[None, :]
    dmat = jnp.exp(jnp.where((rel >= 0)[None], rel[None] * log_gamma[:, None, None], -jnp.inf))
    to_end = jnp.exp((CHUNK - 1 - idx)[None, :] * log_gamma[:, None])
    from_start = jnp.exp((idx + 1.0)[None, :] * log_gamma[:, None])
    toend = jnp.repeat(to_end.T, RET_DK, axis=1)
    fstart = jnp.repeat(from_start.T, RET_DV, axis=1)
    cdec = jnp.broadcast_to(jnp.repeat(jnp.exp(CHUNK * log_gamma), RET_DK)[:, None], (RET_QK_WIDTH, RET_DV))
    return log_gamma, dmat, toend, fstart, cdec


def _np01(a):
    return jnp.asarray(np.asarray(a, np.float32), BF16)


def kernel(x_prompt, x_sample, cache_sb_k, cache_sb_v, page_table, state_ssd_conv, state_ssd, state_ret,
           norm_mix, w_in, conv_w, conv_b, dt_bias, a_log, d_skip, ssd_norm_w, sb_q_norm, sb_k_norm,
           sb_bias, ret_norm_w, w_gate, b_gate, w_br_ssd, w_br_sb, w_br_ret, w_out, norm_mlp, w_up, w_down):
    b_p, seq, d_model = x_prompt.shape
    b_s, t_new, _ = x_sample.shape
    depth = w_in.shape[0]
    n_pages = page_table.shape[1]
    past = n_pages * PAGE_SIZE
    assert b_p == 1 and t_new == 1, "one prompt sequence and one new token per sample sequence"
    assert seq % SB_TQ == 0 and b_s % 8 == 0 and b_s <= LANES and n_pages % DEC_PAGES == 0

    ii = np.arange(CHUNK)
    tril = _np01(ii[:, None] >= ii[None, :])
    triu = _np01(ii[:, None] <= ii[None, :])
    neg_suffix = lambda m: _np01(-(np.arange(m)[:, None] >= np.arange(m)[None, :]).astype(np.float32))
    uneg_page, uneg_blk = neg_suffix(PAGE_SIZE), neg_suffix(SB_KEYS)
    tail_np = np.zeros((LANES, 2 * SB_KEYS), np.float32)
    tail_np[0:2, :SB_KEYS] = 1.0
    tail_np[2:4, SB_KEYS:] = 1.0
    tail = _np01(tail_np)
    hh = np.arange(SB_WIDTH) // SB_HEAD_DIM
    seg = _np01(hh[:, None] == hh[None, :])
    head_of = np.arange(HEAD_ROWS)[:, None] == hh[None, :]
    diag = jnp.asarray(head_of, F32)
    log_gamma, dmat, toend, fstart, cdec = _ret_tables()
    cos_p, sin_p = _rope_tables(jnp.arange(seq, dtype=jnp.int32))
    cos_s, sin_s = _rope_tables(past + jnp.arange(t_new, dtype=jnp.int32))
    gam = jnp.tile(jnp.exp(log_gamma), b_s).reshape(b_s * RET_HEADS, 1, 1)

    sizes = [SSD_WIDTH, SSD_CONV_DIM, SSD_HEADS, SB_WIDTH, SB_WIDTH, SB_WIDTH,
             RET_QK_WIDTH, RET_QK_WIDTH, RET_V_WIDTH, RET_V_WIDTH]
    offs = np.concatenate([[0], np.cumsum(sizes)])
    cols = {name: slice(int(offs[j]), int(offs[j + 1]))
            for j, name in enumerate(["z", "xbc", "dt", "sq", "sk", "sv", "rq", "rk", "rv", "rg"])}
    pt_flat = page_table.reshape(-1)
    page_view = lambda c: jnp.transpose(c, (0, 1, 3, 4, 2)).reshape(depth * c.shape[1], SB_WIDTH, PAGE_SIZE)
    ckt, cvt = page_view(cache_sb_k), page_view(cache_sb_v)

    xp = x_prompt.reshape(seq, d_model)
    xs = x_sample.reshape(b_s, d_model)
    outs = [[] for _ in range(10)]
    for l in range(depth):
        wl = w_in[l]
        w_main = jnp.concatenate([wl[:, cols[c]] for c in ("z", "xbc", "sq", "sv", "rq", "rk", "rv", "rg")],
                                 axis=1).astype(BF16)
        w_kvt = jnp.concatenate([wl[:, cols["sk"]], wl[:, cols["sv"]]], axis=1).T.astype(BF16)
        w_dtc = jnp.pad(wl[:, cols["dt"]], ((0, 0), (0, LANES - SSD_HEADS))).astype(BF16)
        w_dtt = w_dtc[:, :HEAD_ROWS].T
        nw = norm_mix[l][None, :]
        qn = jnp.tile(sb_q_norm[l], SB_HEADS)[None, :]
        kn_col = jnp.tile(sb_k_norm[l], SB_HEADS)[:, None]
        a_neg = -jnp.exp(a_log[l])
        pad_h = lambda v: jnp.pad(v, (0, LANES - SSD_HEADS))
        dtb_row, aneg_row = pad_h(dt_bias[l])[None, :], pad_h(a_neg)[None, :]
        dtb_col, aneg_col = pad_h(dt_bias[l])[:HEAD_ROWS, None], pad_h(a_neg)[:HEAD_ROWS, None]
        dskip = jnp.repeat(d_skip[l], SSD_HEAD_DIM)[None, :]
        snw, rnw = ssd_norm_w[l][None, :], ret_norm_w[l][None, :]
        cw, cb = conv_w[l], conv_b[l][None, :]
        wg, bg = w_gate[l].astype(BF16), b_gate[l][None, :]
        wa, wb, wc = w_br_ssd[l].astype(BF16), w_br_sb[l].astype(BF16), w_br_ret[l].astype(BF16)
        wo, wu, wd = w_out[l].astype(BF16), w_up[l].astype(BF16), w_down[l].astype(BF16)
        nm = norm_mlp[l][None, :]
        kv_out = lambda t, b, rows: jnp.transpose(t.reshape(SB_HEADS, SB_HEAD_DIM, b, rows), (2, 3, 0, 1))

        bias2 = sb_bias[l] * LOG2E
        b_hi = bias2.astype(BF16)
        b_lo = (bias2 - b_hi.astype(F32)).astype(BF16)
        qbias = jnp.pad(jnp.stack([b_hi, b_lo], axis=1).reshape(SB_HEADS // 2, 4),
                        ((0, 0), (0, LANES - 4))).reshape(1, SB_WIDTH)
        p = _in_proj(xp, nw, w_main, w_kvt, w_dtc, w_dtt, qn, kn_col, seg, qbias)
        o_ssd, st_ssd = _ssd_prompt(p, cw, cb, dtb_row, dtb_col, aneg_row, aneg_col, dskip, snw, tril, triu)
        z_max = (SB_HEAD_DIM ** 0.5 * LOG2E * 1.02 * jnp.max(jnp.abs(sb_q_norm[l])) * jnp.max(jnp.abs(sb_k_norm[l]))
                 + bias2 + 0.1)
        dead = -151.0 - z_max
        o_sb = _sb_prompt(dead, p["qp"], p["kblk"], p["vb"], tail, uneg_blk, SB_TQ)
        o_ret, st_ret = _ret_prompt(p, cos_p, sin_p, dmat, toend, fstart, cdec, rnw)
        xp = _mlp(_merge(xp, o_ssd, o_sb, o_ret, nw, wg, bg, wa, wb, wc, wo), nm, wu, wd)
        outs[0].append(kv_out(p["kt"], b_p, seq))
        outs[1].append(kv_out(p["vt"], b_p, seq))
        outs[4].append(p["xbc"][seq - (SSD_CONV - 1):].reshape(b_p, SSD_CONV - 1, SSD_CONV_DIM))
        outs[6].append(st_ssd.reshape(b_p, SSD_HEADS, SSD_HEAD_DIM, SSD_STATE))
        outs[8].append(st_ret.reshape(b_p, RET_HEADS, RET_DK, RET_DV))

        s = _in_proj(xs, nw, w_main, w_kvt, w_dtc, w_dtt, qn, kn_col, seg)
        cp = jnp.swapaxes(state_ssd_conv[l], 0, 1)
        xc, dt, da, rq, rk = _dec_prep(s["xbc"], cp, cw, cb, s["dtc"], dtb_row, aneg_row, s["rq"], s["rk"],
                                       cos_s, sin_s)
        x_ssd = xc[:, :SSD_WIDTH]
        per_head = lambda m: jnp.repeat(m.reshape(b_s, SSD_GROUPS, 1, SSD_STATE), SSD_HEADS // SSD_GROUPS,
                                        axis=1).reshape(b_s * SSD_HEADS, 1, SSD_STATE)
        bh = per_head(xc[:, SSD_WIDTH:SSD_WIDTH + SSD_GROUPS * SSD_STATE])
        ch = per_head(xc[:, SSD_WIDTH + SSD_GROUPS * SSD_STATE:])
        s_new, y_col, r_new, yr = _dec_state(
            state_ssd[l].reshape(b_s * SSD_HEADS, SSD_HEAD_DIM, SSD_STATE),
            x_ssd.reshape(b_s * SSD_HEADS, SSD_HEAD_DIM, 1),
            dt[:, :SSD_HEADS].reshape(b_s * SSD_HEADS, 1, 1), da[:, :SSD_HEADS].reshape(b_s * SSD_HEADS, 1, 1),
            bh, ch,
            state_ret[l].reshape(b_s * RET_HEADS, RET_DK, RET_DV),
            rk.reshape(b_s * RET_HEADS, RET_DK, 1), rq.reshape(b_s * RET_HEADS, RET_DK, 1),
            s["rv"].reshape(b_s * RET_HEADS, 1, RET_DV), gam, 8)
        o_ssd_s, o_ret_s = _dec_post(y_col.reshape(b_s, SSD_WIDTH), x_ssd, dskip, s["z"], snw,
                                     yr.reshape(b_s, RET_V_WIDTH), s["rg"], rnw)
        qbdt = jnp.where(head_of[None], s["qb"][:, None, :], jnp.zeros((), BF16))
        bias_col = pad_h(sb_bias[l])[:HEAD_ROWS, None]
        pad_b = lambda t: jnp.pad(t, ((0, 0), (0, LANES - b_s)))
        o_sb_s = _sb_decode(pt_flat, ckt, cvt, l, bias_col, qbdt, pad_b(s["kt"]), pad_b(s["vt"]), uneg_page, diag,
                            DEC_PAGES).reshape(b_s, SB_WIDTH)
        xs = _mlp(_merge(xs, o_ssd_s, o_sb_s, o_ret_s, nw, wg, bg, wa, wb, wc, wo), nm, wu, wd)
        outs[2].append(kv_out(s["kt"], b_s, t_new))
        outs[3].append(kv_out(s["vt"], b_s, t_new))
        outs[5].append(jnp.concatenate([state_ssd_conv[l][:, 1:], s["xbc"][:, None, :]], axis=1))
        outs[7].append(s_new.reshape(b_s, SSD_HEADS, SSD_HEAD_DIM, SSD_STATE))
        outs[9].append(r_new.reshape(b_s, RET_HEADS, RET_DK, RET_DV))

    stacked = [jnp.stack(o) for o in outs]
    return (xp.reshape(b_p, seq, d_model), xs.reshape(b_s, t_new, d_model), *stacked)
```

```python
import functools

import numpy as np
import jax
import jax.numpy as jnp
from jax import lax
from jax.experimental import pallas as pl
from jax.experimental.pallas import tpu as pltpu

F32 = jnp.float32
BF16 = jnp.bfloat16

SSD_HEAD_DIM = 64
SSD_HEADS = 8
SSD_GROUPS = 2
SSD_STATE = 128
SSD_WIDTH = SSD_HEADS * SSD_HEAD_DIM
SSD_CONV = 4
SSD_CONV_DIM = SSD_WIDTH + 2 * SSD_GROUPS * SSD_STATE
SB_HEADS = 8
SB_HEAD_DIM = 64
SB_WIDTH = SB_HEADS * SB_HEAD_DIM
RET_HEADS = 4
RET_DK = 64
RET_DV = 128
RET_QK_WIDTH = RET_HEADS * RET_DK
RET_V_WIDTH = RET_HEADS * RET_DV
ROPE_BASE = 10000.0
N_BRANCH = 3
EPS = 1e-6
LOG2E = 1.4426950408889634
CHUNK = 128
PAGE_SIZE = 128
SB_KEYS = 256
SB_TQ = 512
SB_ROWS = 256
SB_MAX_LOGIT2 = 126.0
DEC_PAGES = 16

LANES = 128
VMEM_LIMIT = 56 * 1024 * 1024

_NT = (((1,), (1,)), ((), ()))
_TN = (((0,), (0,)), ((), ()))


def _cparams(*sem):
    return pltpu.CompilerParams(dimension_semantics=sem, vmem_limit_bytes=VMEM_LIMIT)


def _split_bf16(a, terms):
    parts, rem = [], a
    for _ in range(terms):
        p = rem.astype(BF16)
        parts.append(p)
        rem = rem - p.astype(F32)
    return parts


def _dot_rhs01(a, m01, terms):
    out = None
    for p in _split_bf16(a, terms):
        d = jnp.dot(p, m01, preferred_element_type=F32)
        out = d if out is None else out + d
    return out


def _dot_lhs01(m01, a, terms):
    out = None
    for p in _split_bf16(a, terms):
        d = jnp.dot(m01, p, preferred_element_type=F32)
        out = d if out is None else out + d
    return out


def _softplus(x):
    return jnp.maximum(x, 0.0) + jnp.log1p(jnp.exp(-jnp.abs(x)))


def _silu(x):
    return x * (1.0 / (1.0 + jnp.exp(-x)))


def _rms(x, w):
    return x * lax.rsqrt(jnp.mean(x * x, axis=-1, keepdims=True) + EPS) * w


def _full(shape):
    nd = len(shape)
    return pl.BlockSpec(shape, lambda *_: (0,) * nd)


_IN_COLS = dict(z=(0, 512), xbc=(512, 1536), sq=(1536, 2048), sv=(2048, 2560),
                rq=(2560, 2816), rk=(2816, 3072), rv=(3072, 3584), rg=(3584, 4096))


def _in_proj_kernel(*refs, prompt):
    (x_ref, nw_ref, w_ref, wkvt_ref, wdtc_ref, wdtt_ref, qn_ref, kn_ref, seg_ref) = refs[:9]
    refs = refs[9:]
    if prompt:
        qbias_ref, refs = refs[0], refs[1:]
    (z_ref, xbc_ref, dtc_ref, dtt_ref, qb_ref, kt_ref, vt_ref, rq_ref, rk_ref, rv_ref, rg_ref) = refs[:11]
    hb = _rms(x_ref[...], nw_ref[...]).astype(BF16)
    tm = hb.shape[0]

    def proj(name):
        lo, hi = _IN_COLS[name]
        return jnp.dot(hb, w_ref[:, lo:hi], preferred_element_type=F32)

    z_ref[...] = proj("z")
    xbc_ref[...] = proj("xbc")
    dtc_ref[...] = jnp.dot(hb, wdtc_ref[...], preferred_element_type=F32)
    dtt_ref[...] = lax.dot_general(wdtt_ref[...], hb, _NT, preferred_element_type=F32)
    sq = proj("sq")
    ms = _dot_rhs01(sq * sq, seg_ref[...], 2) * (1.0 / SB_HEAD_DIM)
    qf = sq * lax.rsqrt(ms + EPS) * qn_ref[...] * (SB_HEAD_DIM ** -0.5)
    qb_ref[...] = qf.astype(BF16)
    kvt = lax.dot_general(wkvt_ref[...], hb, _NT, preferred_element_type=F32)
    k3 = kvt[:SB_WIDTH].reshape(SB_HEADS, SB_HEAD_DIM, tm)
    k3 = k3 * lax.rsqrt(jnp.mean(k3 * k3, axis=1, keepdims=True) + EPS)
    kt = k3.reshape(SB_WIDTH, tm) * kn_ref[...]
    kt_ref[...] = kt
    vt_ref[...] = kvt[SB_WIDTH:]
    rq_ref[...] = proj("rq")
    rk_ref[...] = proj("rk")
    rv_ref[...] = proj("rv")
    rg_ref[...] = proj("rg")
    if prompt:
        qp_ref, kblk_ref, vb_ref = refs[11:]
        ktb = kt.astype(BF16)
        qb = (qf * LOG2E).astype(BF16)
        for p in range(SB_HEADS // 2):
            qp_ref[:, 2 * p * LANES:(2 * p + 1) * LANES] = qb[:, p * LANES:(p + 1) * LANES]
            qp_ref[:, (2 * p + 1) * LANES:(2 * p + 2) * LANES] = jnp.broadcast_to(
                qbias_ref[:, p * LANES:(p + 1) * LANES], (tm, LANES))
            for j in range(tm // SB_KEYS):
                kblk_ref[p, j] = ktb[p * LANES:(p + 1) * LANES, j * SB_KEYS:(j + 1) * SB_KEYS]
        vb_ref[...] = proj("sv").astype(BF16)


def _in_proj(x, nw, w_main, w_kvt, w_dtc, w_dtt, qn, kn_col, seg, qbias=None):
    n, d = x.shape
    tm = min(n, 256)
    prompt = qbias is not None
    row = lambda c: pl.BlockSpec((tm, c), lambda i: (i, 0))
    colmajor = lambda r: pl.BlockSpec((r, tm), lambda i: (0, i))
    outs = [("z", (n, 512), F32, row(512)), ("xbc", (n, 1024), F32, row(1024)),
            ("dtc", (n, LANES), F32, row(LANES)), ("dtt", (16, n), F32, colmajor(16)),
            ("qb", (n, 512), BF16, row(512)), ("kt", (SB_WIDTH, n), F32, colmajor(SB_WIDTH)),
            ("vt", (SB_WIDTH, n), F32, colmajor(SB_WIDTH)),
            ("rq", (n, 256), F32, row(256)), ("rk", (n, 256), F32, row(256)),
            ("rv", (n, 512), F32, row(512)), ("rg", (n, 512), F32, row(512))]
    args = [x, nw, w_main, w_kvt, w_dtc, w_dtt, qn, kn_col, seg]
    if prompt:
        npair, per = SB_HEADS // 2, tm // SB_KEYS
        outs += [("qp", (n, 2 * SB_WIDTH), BF16, row(2 * SB_WIDTH)),
                 ("kblk", (npair, n // SB_KEYS, LANES, SB_KEYS), BF16,
                  pl.BlockSpec((npair, per, LANES, SB_KEYS), lambda i: (0, i, 0, 0))),
                 ("vb", (n, 512), BF16, row(512))]
        args.append(qbias)
    res = pl.pallas_call(
        functools.partial(_in_proj_kernel, prompt=prompt),
        grid=(n // tm,),
        in_specs=[row(d)] + [_full(a.shape) for a in args[1:]],
        out_specs=[o[3] for o in outs],
        out_shape=[jax.ShapeDtypeStruct(o[1], o[2]) for o in outs],
        compiler_params=_cparams("arbitrary"),
    )(*args)
    return dict(zip([o[0] for o in outs], res))


def _ssd_prompt_kernel(z_ref, xbc_ref, dtc_ref, dtt_ref, cw_ref, cb_ref, dtb_row_ref, dtb_col_ref,
                       aneg_row_ref, aneg_col_ref, dskip_ref, nw_ref, tril_ref, triu_ref,
                       o_ref, st_out_ref, xbuf, st):
    c = CHUNK
    xbuf[8:8 + c, :] = xbc_ref[...]
    acc = cb_ref[...] + cw_ref[SSD_CONV - 1:SSD_CONV, :] * xbuf[8:8 + c, :]
    for k in range(SSD_CONV - 1):
        off = 8 - (SSD_CONV - 1) + k
        acc = acc + cw_ref[k:k + 1, :] * xbuf[off:off + c, :]
    xbuf[0:8, :] = xbuf[c:c + 8, :]
    xc = _silu(acc)
    xs = xc[:, :SSD_WIDTH]

    dt_c = _softplus(dtc_ref[...] + dtb_row_ref[...])
    dt_t = _softplus(dtt_ref[...] + dtb_col_ref[...])
    cum_c = _dot_lhs01(tril_ref[...], dt_c * aneg_row_ref[...], 3)
    cum_t = _dot_rhs01(dt_t * aneg_col_ref[...], triu_ref[...], 3)

    row = lax.broadcasted_iota(jnp.int32, (c, c), 0)
    col = lax.broadcasted_iota(jnp.int32, (c, c), 1)
    causal = row >= col
    first_half = lax.broadcasted_iota(jnp.int32, (c, LANES), 1) < SSD_HEAD_DIM

    ys = []
    for pair in range(SSD_HEADS // 2):
        g = pair // 2
        h0, h1 = 2 * pair, 2 * pair + 1
        bm = xc[:, SSD_WIDTH + g * SSD_STATE:SSD_WIDTH + (g + 1) * SSD_STATE].astype(BF16)
        cm = xc[:, SSD_WIDTH + (SSD_GROUPS + g) * SSD_STATE:
                SSD_WIDTH + (SSD_GROUPS + g + 1) * SSD_STATE].astype(BF16)
        cb = lax.dot_general(cm, bm, _NT, preferred_element_type=F32)
        xs_p = xs[:, pair * LANES:(pair + 1) * LANES]
        cc0, cc1 = cum_c[:, h0:h0 + 1], cum_c[:, h1:h1 + 1]
        dt_p = jnp.where(first_half, dt_c[:, h0:h0 + 1], dt_c[:, h1:h1 + 1])
        cc_p = jnp.where(first_half, cc0, cc1)
        last_p = jnp.where(first_half[0:1, :], cum_c[c - 1:c, h0:h0 + 1], cum_c[c - 1:c, h1:h1 + 1])
        xdt = xs_p * dt_p
        xdt_b = xdt.astype(BF16)
        yd = []
        for h, cc in ((h0, cc0), (h1, cc1)):
            seg = cc - cum_t[h:h + 1, :]
            decay = jnp.exp(jnp.where(causal, seg, -jnp.inf))
            yd.append(jnp.dot((cb * decay).astype(BF16), xdt_b, preferred_element_type=F32))
        y_diag = jnp.where(first_half, yd[0], yd[1])
        ent = st[pair * LANES:(pair + 1) * LANES, :]
        y_off = lax.dot_general(cm, ent.astype(BF16), _NT, preferred_element_type=F32) * jnp.exp(cc_p)
        xw = (xdt * jnp.exp(last_p - cc_p)).astype(BF16)
        contrib = lax.dot_general(xw, bm, _TN, preferred_element_type=F32)
        dec0 = jnp.exp(cum_c[c - 1:c, h0:h0 + 1])
        dec1 = jnp.exp(cum_c[c - 1:c, h1:h1 + 1])
        upper = lax.broadcasted_iota(jnp.int32, (LANES, SSD_STATE), 0) < SSD_HEAD_DIM
        st[pair * LANES:(pair + 1) * LANES, :] = ent * jnp.where(upper, dec0, dec1) + contrib
        ys.append(y_diag + y_off + dskip_ref[:, pair * LANES:(pair + 1) * LANES] * xs_p)

    y = jnp.concatenate(ys, axis=-1)
    u = y * _silu(z_ref[...])
    gw = SSD_WIDTH // SSD_GROUPS
    outs = []
    for g in range(SSD_GROUPS):
        ug = u[:, g * gw:(g + 1) * gw]
        outs.append(ug * lax.rsqrt(jnp.mean(ug * ug, axis=-1, keepdims=True) + EPS))
    o_ref[...] = jnp.concatenate(outs, axis=-1) * nw_ref[...]
    st_out_ref[...] = st[...]


def _rotate(x, cosf, sinf):
    w = x.shape[-1]
    half = RET_DK // 2
    ahead = pltpu.roll(x, w - half, 1)
    behind = pltpu.roll(x, half, 1)
    lane = lax.broadcasted_iota(jnp.int32, x.shape, 1) % RET_DK
    return x * cosf + jnp.where(lane < half, ahead, behind) * sinf


def _ret_prompt_kernel(rq_ref, rk_ref, rv_ref, rg_ref, cos_ref, sin_ref, dmat_ref, toend_ref, fstart_ref,
                       cdec_ref, nw_ref, o_ref, st_out_ref, st):
    c = CHUNK
    cosf, sinf = cos_ref[...], sin_ref[...]
    q = _rotate(rq_ref[...], cosf, sinf)
    k = _rotate(rk_ref[...], cosf, sinf) * (RET_DK ** -0.5)
    kw = (k * toend_ref[...]).astype(BF16)
    kb = k.astype(BF16)
    first_half = lax.broadcasted_iota(jnp.int32, (c, LANES), 1) < RET_DK
    ys = []
    for h in range(RET_HEADS):
        pair, odd = h // 2, h % 2
        sl = slice(pair * LANES, (pair + 1) * LANES)
        qm = jnp.where(first_half != bool(odd), q[:, sl], 0.0).astype(BF16)
        v = rv_ref[:, h * RET_DV:(h + 1) * RET_DV].astype(BF16)
        scores = lax.dot_general(qm, kb[:, sl], _NT, preferred_element_type=F32) * dmat_ref[h]
        y = jnp.dot(scores.astype(BF16), v, preferred_element_type=F32)
        ent = st[pair * LANES:(pair + 1) * LANES, :]
        y = y + jnp.dot(qm, ent.astype(BF16), preferred_element_type=F32) * fstart_ref[:, h * RET_DV:(h + 1) * RET_DV]
        contrib = lax.dot_general(kw[:, sl], v, _TN, preferred_element_type=F32)
        r0 = h * RET_DK
        st[r0:r0 + RET_DK, :] = (st[r0:r0 + RET_DK, :] * cdec_ref[r0:r0 + RET_DK, :]
                                 + contrib[odd * RET_DK:(odd + 1) * RET_DK, :])
        yc = y - jnp.mean(y, axis=-1, keepdims=True)
        ys.append(yc * lax.rsqrt(jnp.mean(yc * yc, axis=-1, keepdims=True) + EPS))
    y = jnp.concatenate(ys, axis=-1) * nw_ref[...]
    o_ref[...] = _silu(rg_ref[...]) * y
    st_out_ref[...] = st[...]


def _scan_kernel(*refs, n_ssd_in, n_ret_in):
    a = n_ssd_in
    b = a + n_ret_in
    xbuf, st_ssd, st_ret = refs[b + 4:]

    @pl.when(pl.program_id(0) == 0)
    def _():
        xbuf[0:8, :] = jnp.zeros((8, SSD_CONV_DIM), F32)
        st_ssd[...] = jnp.zeros_like(st_ssd)
        st_ret[...] = jnp.zeros_like(st_ret)

    _ssd_prompt_kernel(*refs[:a], *refs[b:b + 2], xbuf, st_ssd)
    _ret_prompt_kernel(*refs[a:b], *refs[b + 2:b + 4], st_ret)


def _scan_prompt(p, ssd_consts, cosf, sinf, ret_consts):
    n = p["z"].shape[0]
    c = CHUNK
    row = lambda w: pl.BlockSpec((c, w), lambda i: (i, 0))
    ssd_in = [p["z"], p["xbc"], p["dtc"], p["dtt"], *ssd_consts]
    ret_in = [p["rq"], p["rk"], p["rv"], p["rg"], cosf, sinf, *ret_consts]
    ssd_specs = ([row(SSD_WIDTH), row(SSD_CONV_DIM), row(LANES), pl.BlockSpec((16, c), lambda i: (0, i))]
                 + [_full(a.shape) for a in ssd_consts])
    ret_specs = ([row(RET_QK_WIDTH), row(RET_QK_WIDTH), row(RET_V_WIDTH), row(RET_V_WIDTH),
                  row(RET_QK_WIDTH), row(RET_QK_WIDTH)] + [_full(a.shape) for a in ret_consts])
    return pl.pallas_call(
        functools.partial(_scan_kernel, n_ssd_in=len(ssd_in), n_ret_in=len(ret_in)),
        grid=(n // c,),
        in_specs=ssd_specs + ret_specs,
        out_specs=[row(SSD_WIDTH), _full((SSD_WIDTH, SSD_STATE)), row(RET_V_WIDTH), _full((RET_QK_WIDTH, RET_DV))],
        out_shape=[jax.ShapeDtypeStruct((n, SSD_WIDTH), F32), jax.ShapeDtypeStruct((SSD_WIDTH, SSD_STATE), F32),
                   jax.ShapeDtypeStruct((n, RET_V_WIDTH), F32), jax.ShapeDtypeStruct((RET_QK_WIDTH, RET_DV), F32)],
        scratch_shapes=[pltpu.VMEM((c + 8, SSD_CONV_DIM), F32), pltpu.VMEM((SSD_WIDTH, SSD_STATE), F32),
                        pltpu.VMEM((RET_QK_WIDTH, RET_DV), F32)],
        compiler_params=_cparams("arbitrary"),
    )(*ssd_in, *ret_in)


def _neg_abs(x):
    sign = jnp.uint32(0x80000000)
    return lax.bitcast_convert_type(lax.bitcast_convert_type(x, jnp.uint32) | sign, F32)


def _sb_weights(z2, uneg, carries, key_off):
    tq, keys = z2.shape[0], z2.shape[1] // 2
    causal = None
    if key_off is not None:
        row = lax.broadcasted_iota(jnp.int32, (tq, keys), 0)
        col = lax.broadcasted_iota(jnp.int32, (tq, keys), 1)
        causal = col + key_off < row
    zs, sums = [], []
    for h in range(2):
        z = z2[:, h * keys:(h + 1) * keys]
        sp = jnp.log(1.0 + jnp.exp2(jnp.minimum(z, SB_MAX_LOGIT2))) * LOG2E
        if causal is not None:
            sp = jnp.where(causal, sp, 0.0)
        zs.append(z)
        sums.append(jnp.dot(sp.astype(BF16), uneg, preferred_element_type=F32))
    a_parts, new_carries = [], []
    for h in range(2):
        z, s, c = zs[h], sums[h], carries[h]
        a = jnp.exp2(z + s + jnp.concatenate([c] * (keys // LANES), axis=1))
        if causal is not None:
            a = jnp.where(causal, a, 0.0)
        a_parts.append(a.astype(BF16))
        new_carries.append(c + jnp.broadcast_to(s[:, 0:1], c.shape))
    return jnp.concatenate(a_parts, axis=1), new_carries


def _sb_prompt_kernel(dead_ref, q_ref, k_ref, v_ref, tail_ref, uneg_ref, o_ref, zbuf, abuf, cbuf, accbuf):
    pair, i = pl.program_id(0), pl.program_id(1)
    tq, keys = q_ref.shape[0], k_ref.shape[2]
    assert tq == 2 * keys
    qp, tail, uneg = q_ref[...], tail_ref[...], uneg_ref[...]
    zk = jnp.zeros((SB_HEAD_DIM, keys), BF16)

    def key_weights(blk):
        k2 = k_ref[blk]
        return jnp.concatenate([jnp.concatenate([k2[:SB_HEAD_DIM], zk], axis=1),
                                jnp.concatenate([zk, k2[SB_HEAD_DIM:]], axis=1), tail], axis=0)

    def value_weights(blk):
        v = v_ref[pl.ds(pl.multiple_of(blk * keys, keys), keys), :]
        lane_lo = lax.broadcasted_iota(jnp.int32, v.shape, 1) < SB_HEAD_DIM
        zero = jnp.zeros_like(v)
        return jnp.concatenate([jnp.where(lane_lo, v, zero), jnp.where(lane_lo, zero, v)], axis=0)

    def scores(blk):
        return jnp.dot(qp, key_weights(blk), preferred_element_type=F32)

    def weighted_values(a, vbd):
        return jnp.dot(a, vbd, preferred_element_type=F32)

    zero = jnp.zeros((tq, LANES), F32)
    hi, lo = 2 * i + 1, 2 * i
    zero_h = zero[keys:]
    z_hi = jnp.dot(qp[keys:], key_weights(hi), preferred_element_type=F32)
    a_hi, c_hi = _sb_weights(z_hi, uneg, [zero_h, zero_h], 0)
    accbuf[:keys, :] = zero_h
    accbuf[keys:, :] = weighted_values(a_hi, value_weights(hi))
    carries = [jnp.concatenate([zero_h, c], axis=0) for c in c_hi]
    a_lo, carries = _sb_weights(scores(lo), uneg, carries, 0)
    abuf[...] = a_lo
    cbuf[0], cbuf[1] = carries
    zbuf[...] = scores(jnp.maximum(lo - 1, 0))

    dead0, dead1 = dead_ref[2 * pair], dead_ref[2 * pair + 1]

    def step(state):
        t, _, _ = state
        done = jnp.logical_and(jnp.max(cbuf[0]) < dead0, jnp.max(cbuf[1]) < dead1)
        hi = 2 * (i - t) - 1
        lo = hi - 1
        kb_lo, kb_next = key_weights(lo), key_weights(jnp.maximum(lo - 1, 0))
        vb_prev, vb_hi = value_weights(lo + 2), value_weights(hi)
        for r in range(tq // SB_ROWS):
            rows = slice(r * SB_ROWS, (r + 1) * SB_ROWS)
            acc = weighted_values(abuf[rows, :], vb_prev)
            a_hi, carries = _sb_weights(zbuf[rows, :], uneg, [cbuf[0, rows, :], cbuf[1, rows, :]], None)
            z_lo = jnp.dot(qp[rows], kb_lo, preferred_element_type=F32)
            acc = acc + weighted_values(a_hi, vb_hi)
            a_lo, carries = _sb_weights(z_lo, uneg, carries, None)
            abuf[rows, :] = a_lo
            cbuf[0, rows, :], cbuf[1, rows, :] = carries
            zbuf[rows, :] = jnp.dot(qp[rows], kb_next, preferred_element_type=F32)
            accbuf[rows, :] += acc
        return t + 1, done.astype(jnp.int32), lo

    _, _, last = lax.while_loop(lambda s: jnp.logical_and(s[0] < i, s[1] == 0), step,
                                (jnp.int32(0), jnp.int32(0), lo))
    o_ref[...] = accbuf[...] + weighted_values(abuf[...], value_weights(last))


def _sb_prompt(dead, qp, kblk, vb, tail, uneg, tq):
    n = qp.shape[0]
    npair = SB_HEADS // 2
    return pl.pallas_call(
        _sb_prompt_kernel,
        grid_spec=pltpu.PrefetchScalarGridSpec(
            num_scalar_prefetch=1,
            grid=(npair, n // tq),
            in_specs=[pl.BlockSpec((tq, 2 * LANES), lambda p, i, d: (i, p)),
                      pl.BlockSpec((None,) + kblk.shape[1:], lambda p, i, d: (p, 0, 0, 0)),
                      pl.BlockSpec((n, LANES), lambda p, i, d: (0, p)),
                      pl.BlockSpec(tail.shape, lambda p, i, d: (0, 0)),
                      pl.BlockSpec(uneg.shape, lambda p, i, d: (0, 0))],
            out_specs=pl.BlockSpec((tq, LANES), lambda p, i, d: (i, p)),
            scratch_shapes=[pltpu.VMEM((tq, 2 * SB_KEYS), F32), pltpu.VMEM((tq, 2 * SB_KEYS), BF16),
                            pltpu.VMEM((2, tq, LANES), F32), pltpu.VMEM((tq, LANES), F32)],
        ),
        out_shape=jax.ShapeDtypeStruct((n, SB_WIDTH), F32),
        compiler_params=_cparams("arbitrary", "arbitrary"),
    )(dead, qp, kblk, vb, tail, uneg)


def _merge_kernel(x_ref, a_ref, b_ref, c_ref, nw_ref, wg_ref, bg_ref, wa_ref, wb_ref, wc_ref, wo_ref, o_ref):
    x = x_ref[...]
    d = x.shape[-1]
    hb = _rms(x, nw_ref[...]).astype(BF16)
    merged = None
    for j, (br_ref, w_ref) in enumerate(((a_ref, wa_ref), (b_ref, wb_ref), (c_ref, wc_ref))):
        gate = jax.nn.sigmoid(jnp.dot(hb, wg_ref[:, j * d:(j + 1) * d], preferred_element_type=F32)
                              + bg_ref[:, j * d:(j + 1) * d])
        t = gate * jnp.dot(br_ref[...].astype(BF16), w_ref[...], preferred_element_type=F32)
        merged = t if merged is None else merged + t
    o_ref[...] = x + jnp.dot(merged.astype(BF16), wo_ref[...], preferred_element_type=F32)


def _merge(x, o_ssd, o_sb, o_ret, nw, wg, bg, wa, wb, wc, wo):
    n, d = x.shape
    tm = min(n, 256)
    row = lambda w: pl.BlockSpec((tm, w), lambda i: (i, 0))
    consts = [nw, wg, bg, wa, wb, wc, wo]
    return pl.pallas_call(
        _merge_kernel,
        grid=(n // tm,),
        in_specs=[row(d), row(o_ssd.shape[1]), row(o_sb.shape[1]), row(o_ret.shape[1])]
                 + [_full(a.shape) for a in consts],
        out_specs=row(d),
        out_shape=jax.ShapeDtypeStruct((n, d), F32),
        compiler_params=_cparams("arbitrary"),
    )(x, o_ssd, o_sb, o_ret, *consts)


def _mlp_kernel(x_ref, nw_ref, wu_ref, wd_ref, o_ref):
    x = x_ref[...]
    hb = _rms(x, nw_ref[...]).astype(BF16)
    up = jnp.maximum(jnp.dot(hb, wu_ref[...], preferred_element_type=F32), 0.0)
    o_ref[...] = x + jnp.dot((up * up).astype(BF16), wd_ref[...], preferred_element_type=F32)


def _mlp(x, nw, wu, wd):
    n, d = x.shape
    tm = min(n, 256)
    row = pl.BlockSpec((tm, d), lambda i: (i, 0))
    return pl.pallas_call(
        _mlp_kernel,
        grid=(n // tm,),
        in_specs=[row, _full(nw.shape), _full(wu.shape), _full(wd.shape)],
        out_specs=row,
        out_shape=jax.ShapeDtypeStruct((n, d), F32),
        compiler_params=_cparams("arbitrary"),
    )(x, nw, wu, wd)


def _dec_prep_kernel(xbc_ref, cp_ref, cw_ref, cb_ref, dtc_ref, dtb_ref, aneg_ref, rq_ref, rk_ref,
                     cos_ref, sin_ref, xc_ref, dt_ref, da_ref, q_ref, k_ref):
    acc = cb_ref[...] + cw_ref[SSD_CONV - 1:SSD_CONV, :] * xbc_ref[...]
    for k in range(SSD_CONV - 1):
        acc = acc + cw_ref[k:k + 1, :] * cp_ref[k]
    xc_ref[...] = _silu(acc)
    dt = _softplus(dtc_ref[...] + dtb_ref[...])
    dt_ref[...] = dt
    da_ref[...] = jnp.exp(dt * aneg_ref[...])
    q_ref[...] = _rotate(rq_ref[...], cos_ref[...], sin_ref[...])
    k_ref[...] = _rotate(rk_ref[...], cos_ref[...], sin_ref[...]) * (RET_DK ** -0.5)


def _dec_prep(xbc, cp, cw, cb, dtc, dtb, aneg, rq, rk, cosr, sinr):
    b = xbc.shape[0]
    args = [xbc, cp, cw, cb, dtc, dtb, aneg, rq, rk, cosr, sinr]
    return pl.pallas_call(
        _dec_prep_kernel,
        grid=(1,),
        in_specs=[_full(a.shape) for a in args],
        out_specs=[_full((b, SSD_CONV_DIM)), _full((b, LANES)), _full((b, LANES)),
                   _full((b, RET_QK_WIDTH)), _full((b, RET_QK_WIDTH))],
        out_shape=[jax.ShapeDtypeStruct((b, SSD_CONV_DIM), F32), jax.ShapeDtypeStruct((b, LANES), F32),
                   jax.ShapeDtypeStruct((b, LANES), F32), jax.ShapeDtypeStruct((b, RET_QK_WIDTH), F32),
                   jax.ShapeDtypeStruct((b, RET_QK_WIDTH), F32)],
        compiler_params=_cparams("arbitrary"),
    )(*args)


def _dec_state_kernel(s_ref, xs_ref, dt_ref, da_ref, bh_ref, ch_ref, r_ref, kc_ref, qc_ref, vr_ref, gam_ref,
                      s_out_ref, y_ref, r_out_ref, yr_ref):
    s_new = s_ref[...] * da_ref[...] + (xs_ref[...] * dt_ref[...]) * bh_ref[...]
    s_out_ref[...] = s_new
    y_ref[...] = jnp.sum(s_new * ch_ref[...], axis=-1, keepdims=True)
    r_new = r_ref[...] * gam_ref[...] + kc_ref[...] * vr_ref[...]
    r_out_ref[...] = r_new
    yr_ref[...] = jnp.sum(qc_ref[...] * r_new, axis=-2, keepdims=True)


def _dec_state(s, xs_col, dt_col, da_col, bh, ch, r, k_col, q_col, v_row, gam, nb):
    b = s.shape[0] // SSD_HEADS
    hs, hr = nb * SSD_HEADS, nb * RET_HEADS
    blk = lambda rows, a, c: pl.BlockSpec((rows, a, c), lambda i: (i, 0, 0))
    return pl.pallas_call(
        _dec_state_kernel,
        grid=(b // nb,),
        in_specs=[blk(hs, SSD_HEAD_DIM, SSD_STATE), blk(hs, SSD_HEAD_DIM, 1), blk(hs, 1, 1), blk(hs, 1, 1),
                  blk(hs, 1, SSD_STATE), blk(hs, 1, SSD_STATE),
                  blk(hr, RET_DK, RET_DV), blk(hr, RET_DK, 1), blk(hr, RET_DK, 1), blk(hr, 1, RET_DV),
                  blk(hr, 1, 1)],
        out_specs=[blk(hs, SSD_HEAD_DIM, SSD_STATE), blk(hs, SSD_HEAD_DIM, 1),
                   blk(hr, RET_DK, RET_DV), blk(hr, 1, RET_DV)],
        out_shape=[jax.ShapeDtypeStruct(s.shape, F32), jax.ShapeDtypeStruct(xs_col.shape, F32),
                   jax.ShapeDtypeStruct(r.shape, F32), jax.ShapeDtypeStruct(v_row.shape, F32)],
        compiler_params=_cparams("arbitrary"),
    )(s, xs_col, dt_col, da_col, bh, ch, r, k_col, q_col, v_row, gam)


def _dec_post_kernel(y_ref, xs_ref, dskip_ref, z_ref, snw_ref, yr_ref, rg_ref, rnw_ref, o_ssd_ref, o_ret_ref):
    u = (y_ref[...] + dskip_ref[...] * xs_ref[...]) * _silu(z_ref[...])
    gw = SSD_WIDTH // SSD_GROUPS
    outs = []
    for g in range(SSD_GROUPS):
        ug = u[:, g * gw:(g + 1) * gw]
        outs.append(ug * lax.rsqrt(jnp.mean(ug * ug, axis=-1, keepdims=True) + EPS))
    o_ssd_ref[...] = jnp.concatenate(outs, axis=-1) * snw_ref[...]
    ys = []
    for h in range(RET_HEADS):
        y = yr_ref[:, h * RET_DV:(h + 1) * RET_DV]
        yc = y - jnp.mean(y, axis=-1, keepdims=True)
        ys.append(yc * lax.rsqrt(jnp.mean(yc * yc, axis=-1, keepdims=True) + EPS))
    o_ret_ref[...] = _silu(rg_ref[...]) * (jnp.concatenate(ys, axis=-1) * rnw_ref[...])


def _dec_post(y, xs, dskip, z, snw, yr, rg, rnw):
    b = y.shape[0]
    args = [y, xs, dskip, z, snw, yr, rg, rnw]
    return pl.pallas_call(
        _dec_post_kernel,
        grid=(1,),
        in_specs=[_full(a.shape) for a in args],
        out_specs=[_full((b, SSD_WIDTH)), _full((b, RET_V_WIDTH))],
        out_shape=[jax.ShapeDtypeStruct((b, SSD_WIDTH), F32), jax.ShapeDtypeStruct((b, RET_V_WIDTH), F32)],
        compiler_params=_cparams("arbitrary"),
    )(*args)


HEAD_ROWS = 16


def _sb_decode_kernel(pt_ref, bias_ref, qbdt_ref, knewt_ref, vnewt_ref, uo_ref, diag_ref, *rest, pages, past):
    k_refs, v_refs = rest[:pages], rest[pages:2 * pages]
    o_ref, acc, carry = rest[2 * pages], rest[2 * pages + 1], rest[2 * pages + 2]
    bi, c = pl.program_id(0), pl.program_id(1)
    bias = bias_ref[...]
    qbdt = qbdt_ref[0]
    uneg = uo_ref[...]

    def sums(kt, visible):
        z = jnp.dot(qbdt, kt.astype(BF16), preferred_element_type=F32) + bias
        sp = jnp.maximum(z, 0.0) + jnp.log(1.0 + jnp.exp(_neg_abs(z)))
        if visible is not None:
            sp = jnp.where(visible, sp, 0.0)
        return z, _dot_rhs01(sp, uneg, 2)

    def weigh(z, s, carry_in, vt, visible):
        a = jnp.exp(z + s + carry_in)
        if visible is not None:
            a = jnp.where(visible, a, 0.0)
        out = lax.dot_general(a.astype(BF16), vt.astype(BF16), _NT, preferred_element_type=F32)
        return out, carry_in + jnp.broadcast_to(s[:, 0:1], carry_in.shape)

    @pl.when(c == 0)
    def _():
        lane = lax.broadcasted_iota(jnp.int32, (HEAD_ROWS, LANES), 1)
        key_pos = jnp.full((HEAD_ROWS, LANES), past, jnp.int32)
        visible = (lane == bi) & (key_pos < past)
        z, s = sums(knewt_ref[...], visible)
        out, cnew = weigh(z, s, jnp.zeros((HEAD_ROWS, LANES), F32), vnewt_ref[...], visible)
        acc[...] = out
        carry[...] = cnew

    zs = [sums(k_ref[...], None) for k_ref in k_refs]
    cur = carry[...]
    total = acc[...]
    for (z, s), v_ref in zip(zs, v_refs):
        out, cur = weigh(z, s, cur, v_ref[...], None)
        total = total + out
    acc[...] = total
    carry[...] = cur

    @pl.when(c == pl.num_programs(1) - 1)
    def _():
        o_ref[0] = jnp.sum(acc[...] * diag_ref[...], axis=0, keepdims=True)


def _sb_decode(page_table, cache_kt, cache_vt, layer, bias_col, qbdt, knewt, vnewt, uo, diag, pages):
    b = qbdt.shape[0]
    n_pages = page_table.shape[0] // b
    n_phys = cache_kt.shape[0] // 2
    nc = n_pages // pages
    base = layer * n_phys

    def page_spec(j):
        def imap(bi, c, pt):
            logical = (nc - 1 - c) * pages + (pages - 1 - j)
            return (base + pt[bi * n_pages + logical], 0, 0)
        return pl.BlockSpec((None, SB_WIDTH, PAGE_SIZE), imap)

    const = lambda a: pl.BlockSpec(a.shape, lambda bi, c, pt: (0,) * a.ndim)
    return pl.pallas_call(
        functools.partial(_sb_decode_kernel, pages=pages, past=n_pages * PAGE_SIZE),
        grid_spec=pltpu.PrefetchScalarGridSpec(
            num_scalar_prefetch=1,
            grid=(b, nc),
            in_specs=[const(bias_col), pl.BlockSpec((1,) + qbdt.shape[1:], lambda bi, c, pt: (bi, 0, 0)),
                      const(knewt), const(vnewt), const(uo), const(diag)]
                     + [page_spec(j) for j in range(pages)] * 2,
            out_specs=pl.BlockSpec((1, 1, SB_WIDTH), lambda bi, c, pt: (bi, 0, 0)),
            scratch_shapes=[pltpu.VMEM((HEAD_ROWS, SB_WIDTH), F32), pltpu.VMEM((HEAD_ROWS, LANES), F32)],
        ),
        out_shape=jax.ShapeDtypeStruct((b, 1, SB_WIDTH), F32),
        compiler_params=_cparams("arbitrary", "arbitrary"),
    )(page_table, bias_col, qbdt, knewt, vnewt, uo, diag, *([cache_kt] * pages), *([cache_vt] * pages))


def _rope_tables(pos):
    half = RET_DK // 2
    inv = ROPE_BASE ** (-jnp.arange(half, dtype=F32) / half)
    ang = pos.astype(F32)[:, None] * inv[None, :]
    cos, sin = jnp.cos(ang), jnp.sin(ang)
    cosf = jnp.tile(jnp.concatenate([cos, cos], axis=-1), (1, RET_HEADS))
    sinf = jnp.tile(jnp.concatenate([-sin, sin], axis=-1), (1, RET_HEADS))
    return cosf, sinf


def _ret_tables():
    log_gamma = jnp.log1p(-jnp.exp2(-5.0 - jnp.arange(RET_HEADS, dtype=F32)))
    idx = jnp.arange(CHUNK, dtype=F32)
    rel = idx[:, None] - idx[None, :]
    dmat = jnp.exp(jnp.where((rel >= 0)[None], rel[None] * log_gamma[:, None, None], -jnp.inf))
    to_end = jnp.exp((CHUNK - 1 - idx)[None, :] * log_gamma[:, None])
    from_start = jnp.exp((idx + 1.0)[None, :] * log_gamma[:, None])
    toend = jnp.repeat(to_end.T, RET_DK, axis=1)
    fstart = jnp.repeat(from_start.T, RET_DV, axis=1)
    cdec = jnp.broadcast_to(jnp.repeat(jnp.exp(CHUNK * log_gamma), RET_DK)[:, None], (RET_QK_WIDTH, RET_DV))
    return log_gamma, dmat, toend, fstart, cdec


def _np01(a):
    return jnp.asarray(np.asarray(a, np.float32), BF16)


def kernel(x_prompt, x_sample, cache_sb_k, cache_sb_v, page_table, state_ssd_conv, state_ssd, state_ret,
           norm_mix, w_in, conv_w, conv_b, dt_bias, a_log, d_skip, ssd_norm_w, sb_q_norm, sb_k_norm,
           sb_bias, ret_norm_w, w_gate, b_gate, w_br_ssd, w_br_sb, w_br_ret, w_out, norm_mlp, w_up, w_down):
    b_p, seq, d_model = x_prompt.shape
    b_s, t_new, _ = x_sample.shape
    depth = w_in.shape[0]
    n_pages = page_table.shape[1]
    past = n_pages * PAGE_SIZE
    assert b_p == 1 and t_new == 1, "one prompt sequence and one new token per sample sequence"
    assert seq % SB_TQ == 0 and b_s % 8 == 0 and b_s <= LANES and n_pages % DEC_PAGES == 0

    ii = np.arange(CHUNK)
    tril = _np01(ii[:, None] >= ii[None, :])
    triu = _np01(ii[:, None] <= ii[None, :])
    neg_suffix = lambda m: _np01(-(np.arange(m)[:, None] >= np.arange(m)[None, :]).astype(np.float32))
    uneg_page, uneg_blk = neg_suffix(PAGE_SIZE), neg_suffix(SB_KEYS)
    tail_np = np.zeros((LANES, 2 * SB_KEYS), np.float32)
    tail_np[0:2, :SB_KEYS] = 1.0
    tail_np[2:4, SB_KEYS:] = 1.0
    tail = _np01(tail_np)
    hh = np.arange(SB_WIDTH) // SB_HEAD_DIM
    seg = _np01(hh[:, None] == hh[None, :])
    head_of = np.arange(HEAD_ROWS)[:, None] == hh[None, :]
    diag = jnp.asarray(head_of, F32)
    log_gamma, dmat, toend, fstart, cdec = _ret_tables()
    cos_p, sin_p = _rope_tables(jnp.arange(seq, dtype=jnp.int32))
    cos_s, sin_s = _rope_tables(past + jnp.arange(t_new, dtype=jnp.int32))
    gam = jnp.tile(jnp.exp(log_gamma), b_s).reshape(b_s * RET_HEADS, 1, 1)

    sizes = [SSD_WIDTH, SSD_CONV_DIM, SSD_HEADS, SB_WIDTH, SB_WIDTH, SB_WIDTH,
             RET_QK_WIDTH, RET_QK_WIDTH, RET_V_WIDTH, RET_V_WIDTH]
    offs = np.concatenate([[0], np.cumsum(sizes)])
    cols = {name: slice(int(offs[j]), int(offs[j + 1]))
            for j, name in enumerate(["z", "xbc", "dt", "sq", "sk", "sv", "rq", "rk", "rv", "rg"])}
    pt_flat = page_table.reshape(-1)
    page_view = lambda c: jnp.transpose(c, (0, 1, 3, 4, 2)).reshape(depth * c.shape[1], SB_WIDTH, PAGE_SIZE)
    ckt, cvt = page_view(cache_sb_k), page_view(cache_sb_v)

    xp = x_prompt.reshape(seq, d_model)
    xs = x_sample.reshape(b_s, d_model)
    outs = [[] for _ in range(10)]
    for l in range(depth):
        wl = w_in[l]
        w_main = jnp.concatenate([wl[:, cols[c]] for c in ("z", "xbc", "sq", "sv", "rq", "rk", "rv", "rg")],
                                 axis=1).astype(BF16)
        w_kvt = jnp.concatenate([wl[:, cols["sk"]], wl[:, cols["sv"]]], axis=1).T.astype(BF16)
        w_dtc = jnp.pad(wl[:, cols["dt"]], ((0, 0), (0, LANES - SSD_HEADS))).astype(BF16)
        w_dtt = w_dtc[:, :HEAD_ROWS].T
        nw = norm_mix[l][None, :]
        qn = jnp.tile(sb_q_norm[l], SB_HEADS)[None, :]
        kn_col = jnp.tile(sb_k_norm[l], SB_HEADS)[:, None]
        a_neg = -jnp.exp(a_log[l])
        pad_h = lambda v: jnp.pad(v, (0, LANES - SSD_HEADS))
        dtb_row, aneg_row = pad_h(dt_bias[l])[None, :], pad_h(a_neg)[None, :]
        dtb_col, aneg_col = pad_h(dt_bias[l])[:HEAD_ROWS, None], pad_h(a_neg)[:HEAD_ROWS, None]
        dskip = jnp.repeat(d_skip[l], SSD_HEAD_DIM)[None, :]
        snw, rnw = ssd_norm_w[l][None, :], ret_norm_w[l][None, :]
        cw, cb = conv_w[l], conv_b[l][None, :]
        wg, bg = w_gate[l].astype(BF16), b_gate[l][None, :]
        wa, wb, wc = w_br_ssd[l].astype(BF16), w_br_sb[l].astype(BF16), w_br_ret[l].astype(BF16)
        wo, wu, wd = w_out[l].astype(BF16), w_up[l].astype(BF16), w_down[l].astype(BF16)
        nm = norm_mlp[l][None, :]
        kv_out = lambda t, b, rows: jnp.transpose(t.reshape(SB_HEADS, SB_HEAD_DIM, b, rows), (2, 3, 0, 1))

        bias2 = sb_bias[l] * LOG2E
        b_hi = bias2.astype(BF16)
        b_lo = (bias2 - b_hi.astype(F32)).astype(BF16)
        qbias = jnp.pad(jnp.stack([b_hi, b_lo], axis=1).reshape(SB_HEADS // 2, 4),
                        ((0, 0), (0, LANES - 4))).reshape(1, SB_WIDTH)
        p = _in_proj(xp, nw, w_main, w_kvt, w_dtc, w_dtt, qn, kn_col, seg, qbias)
        o_ssd, st_ssd, o_ret, st_ret = _scan_prompt(
            p, [cw, cb, dtb_row, dtb_col, aneg_row, aneg_col, dskip, snw, tril, triu],
            cos_p, sin_p, [dmat, toend, fstart, cdec, rnw])
        z_max = (SB_HEAD_DIM ** 0.5 * LOG2E * 1.02 * jnp.max(jnp.abs(sb_q_norm[l])) * jnp.max(jnp.abs(sb_k_norm[l]))
                 + bias2 + 0.1)
        dead = -151.0 - z_max
        o_sb = _sb_prompt(dead, p["qp"], p["kblk"], p["vb"], tail, uneg_blk, SB_TQ)
        xp = _mlp(_merge(xp, o_ssd, o_sb, o_ret, nw, wg, bg, wa, wb, wc, wo), nm, wu, wd)
        outs[0].append(kv_out(p["kt"], b_p, seq))
        outs[1].append(kv_out(p["vt"], b_p, seq))
        outs[4].append(p["xbc"][seq - (SSD_CONV - 1):].reshape(b_p, SSD_CONV - 1, SSD_CONV_DIM))
        outs[6].append(st_ssd.reshape(b_p, SSD_HEADS, SSD_HEAD_DIM, SSD_STATE))
        outs[8].append(st_ret.reshape(b_p, RET_HEADS, RET_DK, RET_DV))

        s = _in_proj(xs, nw, w_main, w_kvt, w_dtc, w_dtt, qn, kn_col, seg)
        cp = jnp.swapaxes(state_ssd_conv[l], 0, 1)
        xc, dt, da, rq, rk = _dec_prep(s["xbc"], cp, cw, cb, s["dtc"], dtb_row, aneg_row, s["rq"], s["rk"],
                                       cos_s, sin_s)
        x_ssd = xc[:, :SSD_WIDTH]
        per_head = lambda m: jnp.repeat(m.reshape(b_s, SSD_GROUPS, 1, SSD_STATE), SSD_HEADS // SSD_GROUPS,
                                        axis=1).reshape(b_s * SSD_HEADS, 1, SSD_STATE)
        bh = per_head(xc[:, SSD_WIDTH:SSD_WIDTH + SSD_GROUPS * SSD_STATE])
        ch = per_head(xc[:, SSD_WIDTH + SSD_GROUPS * SSD_STATE:])
        s_new, y_col, r_new, yr = _dec_state(
            state_ssd[l].reshape(b_s * SSD_HEADS, SSD_HEAD_DIM, SSD_STATE),
            x_ssd.reshape(b_s * SSD_HEADS, SSD_HEAD_DIM, 1),
            dt[:, :SSD_HEADS].reshape(b_s * SSD_HEADS, 1, 1), da[:, :SSD_HEADS].reshape(b_s * SSD_HEADS, 1, 1),
            bh, ch,
            state_ret[l].reshape(b_s * RET_HEADS, RET_DK, RET_DV),
            rk.reshape(b_s * RET_HEADS, RET_DK, 1), rq.reshape(b_s * RET_HEADS, RET_DK, 1),
            s["rv"].reshape(b_s * RET_HEADS, 1, RET_DV), gam, 8)
        o_ssd_s, o_ret_s = _dec_post(y_col.reshape(b_s, SSD_WIDTH), x_ssd, dskip, s["z"], snw,
                                     yr.reshape(b_s, RET_V_WIDTH), s["rg"], rnw)
        qbdt = jnp.where(head_of[None], s["qb"][:, None, :], jnp.zeros((), BF16))
        bias_col = pad_h(sb_bias[l])[:HEAD_ROWS, None]
        pad_b = lambda t: jnp.pad(t, ((0, 0), (0, LANES - b_s)))
        o_sb_s = _sb_decode(pt_flat, ckt, cvt, l, bias_col, qbdt, pad_b(s["kt"]), pad_b(s["vt"]), uneg_page, diag,
                            DEC_PAGES).reshape(b_s, SB_WIDTH)
        xs = _mlp(_merge(xs, o_ssd_s, o_sb_s, o_ret_s, nw, wg, bg, wa, wb, wc, wo), nm, wu, wd)
        outs[2].append(kv_out(s["kt"], b_s, t_new))
        outs[3].append(kv_out(s["vt"], b_s, t_new))
        outs[5].append(jnp.concatenate([state_ssd_conv[l][:, 1:], s["xbc"][:, None, :]], axis=1))
        outs[7].append(s_new.reshape(b_s, SSD_HEADS, SSD_HEAD_DIM, SSD_STATE))
        outs[9].append(r_new.reshape(b_s, RET_HEADS, RET_DK, RET_DV))

    stacked = [jnp.stack(o) for o in outs]
    return (xp.reshape(b_p, seq, d_model), xs.reshape(b_s, t_new, d_model), *stacked)
```

```python
import functools

import numpy as np
import jax
import jax.numpy as jnp
from jax import lax
from jax.experimental import pallas as pl
from jax.experimental.pallas import tpu as pltpu

F32 = jnp.float32
BF16 = jnp.bfloat16

SSD_HEAD_DIM = 64
SSD_HEADS = 8
SSD_GROUPS = 2
SSD_STATE = 128
SSD_WIDTH = SSD_HEADS * SSD_HEAD_DIM
SSD_CONV = 4
SSD_CONV_DIM = SSD_WIDTH + 2 * SSD_GROUPS * SSD_STATE
SB_HEADS = 8
SB_HEAD_DIM = 64
SB_WIDTH = SB_HEADS * SB_HEAD_DIM
RET_HEADS = 4
RET_DK = 64
RET_DV = 128
RET_QK_WIDTH = RET_HEADS * RET_DK
RET_V_WIDTH = RET_HEADS * RET_DV
ROPE_BASE = 10000.0
N_BRANCH = 3
EPS = 1e-6
LOG2E = 1.4426950408889634
CHUNK = 128
PAGE_SIZE = 128
SB_KEYS = 256
SB_TQ = 512
SB_ROWS = 256
SB_MAX_LOGIT2 = 126.0
DEC_PAGES = 16

LANES = 128
VMEM_LIMIT = 56 * 1024 * 1024

_NT = (((1,), (1,)), ((), ()))
_TN = (((0,), (0,)), ((), ()))


def _cparams(*sem):
    return pltpu.CompilerParams(dimension_semantics=sem, vmem_limit_bytes=VMEM_LIMIT)


def _split_bf16(a, terms):
    parts, rem = [], a
    for _ in range(terms):
        p = rem.astype(BF16)
        parts.append(p)
        rem = rem - p.astype(F32)
    return parts


def _dot_rhs01(a, m01, terms):
    out = None
    for p in _split_bf16(a, terms):
        d = jnp.dot(p, m01, preferred_element_type=F32)
        out = d if out is None else out + d
    return out


def _dot_lhs01(m01, a, terms):
    out = None
    for p in _split_bf16(a, terms):
        d = jnp.dot(m01, p, preferred_element_type=F32)
        out = d if out is None else out + d
    return out


def _softplus(x):
    return jnp.maximum(x, 0.0) + jnp.log1p(jnp.exp(-jnp.abs(x)))


def _silu(x):
    return x * (1.0 / (1.0 + jnp.exp(-x)))


def _rms(x, w):
    return x * lax.rsqrt(jnp.mean(x * x, axis=-1, keepdims=True) + EPS) * w


def _full(shape):
    nd = len(shape)
    return pl.BlockSpec(shape, lambda *_: (0,) * nd)


_IN_COLS = dict(z=(0, 512), xbc=(512, 1536), sq=(1536, 2048), sv=(2048, 2560),
                rq=(2560, 2816), rk=(2816, 3072), rv=(3072, 3584), rg=(3584, 4096))


def _in_proj_kernel(*refs, prompt):
    (x_ref, nw_ref, w_ref, wkvt_ref, wdtc_ref, wdtt_ref, qn_ref, kn_ref, seg_ref) = refs[:9]
    refs = refs[9:]
    if prompt:
        qbias_ref, refs = refs[0], refs[1:]
    (z_ref, xbc_ref, dtc_ref, dtt_ref, qb_ref, kt_ref, vt_ref, rq_ref, rk_ref, rv_ref, rg_ref) = refs[:11]
    hb = _rms(x_ref[...], nw_ref[...]).astype(BF16)
    tm = hb.shape[0]

    def proj(name):
        lo, hi = _IN_COLS[name]
        return jnp.dot(hb, w_ref[:, lo:hi], preferred_element_type=F32)

    z_ref[...] = proj("z")
    xbc_ref[...] = proj("xbc")
    dtc_ref[...] = jnp.dot(hb, wdtc_ref[...], preferred_element_type=F32)
    dtt_ref[...] = lax.dot_general(wdtt_ref[...], hb, _NT, preferred_element_type=F32)
    sq = proj("sq")
    ms = _dot_rhs01(sq * sq, seg_ref[...], 2) * (1.0 / SB_HEAD_DIM)
    qf = sq * lax.rsqrt(ms + EPS) * qn_ref[...] * (SB_HEAD_DIM ** -0.5)
    qb_ref[...] = qf.astype(BF16)
    kvt = lax.dot_general(wkvt_ref[...], hb, _NT, preferred_element_type=F32)
    k3 = kvt[:SB_WIDTH].reshape(SB_HEADS, SB_HEAD_DIM, tm)
    k3 = k3 * lax.rsqrt(jnp.mean(k3 * k3, axis=1, keepdims=True) + EPS)
    kt = k3.reshape(SB_WIDTH, tm) * kn_ref[...]
    kt_ref[...] = kt
    vt_ref[...] = kvt[SB_WIDTH:]
    rq_ref[...] = proj("rq")
    rk_ref[...] = proj("rk")
    rv_ref[...] = proj("rv")
    rg_ref[...] = proj("rg")
    if prompt:
        qp_ref, kblk_ref, vb_ref = refs[11:]
        ktb = kt.astype(BF16)
        qb = (qf * LOG2E).astype(BF16)
        for p in range(SB_HEADS // 2):
            qp_ref[:, 2 * p * LANES:(2 * p + 1) * LANES] = qb[:, p * LANES:(p + 1) * LANES]
            qp_ref[:, (2 * p + 1) * LANES:(2 * p + 2) * LANES] = jnp.broadcast_to(
                qbias_ref[:, p * LANES:(p + 1) * LANES], (tm, LANES))
            for j in range(tm // SB_KEYS):
                kblk_ref[p, j] = ktb[p * LANES:(p + 1) * LANES, j * SB_KEYS:(j + 1) * SB_KEYS]
        vb_ref[...] = proj("sv").astype(BF16)


def _in_proj(x, nw, w_main, w_kvt, w_dtc, w_dtt, qn, kn_col, seg, qbias=None):
    n, d = x.shape
    tm = min(n, 256)
    prompt = qbias is not None
    row = lambda c: pl.BlockSpec((tm, c), lambda i: (i, 0))
    colmajor = lambda r: pl.BlockSpec((r, tm), lambda i: (0, i))
    outs = [("z", (n, 512), F32, row(512)), ("xbc", (n, 1024), F32, row(1024)),
            ("dtc", (n, LANES), F32, row(LANES)), ("dtt", (16, n), F32, colmajor(16)),
            ("qb", (n, 512), BF16, row(512)), ("kt", (SB_WIDTH, n), F32, colmajor(SB_WIDTH)),
            ("vt", (SB_WIDTH, n), F32, colmajor(SB_WIDTH)),
            ("rq", (n, 256), F32, row(256)), ("rk", (n, 256), F32, row(256)),
            ("rv", (n, 512), F32, row(512)), ("rg", (n, 512), F32, row(512))]
    args = [x, nw, w_main, w_kvt, w_dtc, w_dtt, qn, kn_col, seg]
    if prompt:
        npair, per = SB_HEADS // 2, tm // SB_KEYS
        outs += [("qp", (n, 2 * SB_WIDTH), BF16, row(2 * SB_WIDTH)),
                 ("kblk", (npair, n // SB_KEYS, LANES, SB_KEYS), BF16,
                  pl.BlockSpec((npair, per, LANES, SB_KEYS), lambda i: (0, i, 0, 0))),
                 ("vb", (n, 512), BF16, row(512))]
        args.append(qbias)
    res = pl.pallas_call(
        functools.partial(_in_proj_kernel, prompt=prompt),
        grid=(n // tm,),
        in_specs=[row(d)] + [_full(a.shape) for a in args[1:]],
        out_specs=[o[3] for o in outs],
        out_shape=[jax.ShapeDtypeStruct(o[1], o[2]) for o in outs],
        compiler_params=_cparams("arbitrary"),
    )(*args)
    return dict(zip([o[0] for o in outs], res))


def _ssd_prompt_kernel(z_ref, xbc_ref, dtc_ref, dtt_ref, cw_ref, cb_ref, dtb_row_ref, dtb_col_ref,
                       aneg_row_ref, aneg_col_ref, dskip_ref, nw_ref, tril_ref, triu_ref,
                       o_ref, st_out_ref, xbuf, st):
    c = CHUNK
    xbuf[8:8 + c, :] = xbc_ref[...]
    acc = cb_ref[...] + cw_ref[SSD_CONV - 1:SSD_CONV, :] * xbuf[8:8 + c, :]
    for k in range(SSD_CONV - 1):
        off = 8 - (SSD_CONV - 1) + k
        acc = acc + cw_ref[k:k + 1, :] * xbuf[off:off + c, :]
    xbuf[0:8, :] = xbuf[c:c + 8, :]
    xc = _silu(acc)
    xs = xc[:, :SSD_WIDTH]

    dt_c = _softplus(dtc_ref[...] + dtb_row_ref[...])
    dt_t = _softplus(dtt_ref[...] + dtb_col_ref[...])
    cum_c = _dot_lhs01(tril_ref[...], dt_c * aneg_row_ref[...], 3)
    cum_t = _dot_rhs01(dt_t * aneg_col_ref[...], triu_ref[...], 3)

    row = lax.broadcasted_iota(jnp.int32, (c, c), 0)
    col = lax.broadcasted_iota(jnp.int32, (c, c), 1)
    causal = row >= col
    first_half = lax.broadcasted_iota(jnp.int32, (c, LANES), 1) < SSD_HEAD_DIM

    ys = []
    for pair in range(SSD_HEADS // 2):
        g = pair // 2
        h0, h1 = 2 * pair, 2 * pair + 1
        bm = xc[:, SSD_WIDTH + g * SSD_STATE:SSD_WIDTH + (g + 1) * SSD_STATE].astype(BF16)
        cm = xc[:, SSD_WIDTH + (SSD_GROUPS + g) * SSD_STATE:
                SSD_WIDTH + (SSD_GROUPS + g + 1) * SSD_STATE].astype(BF16)
        cb = lax.dot_general(cm, bm, _NT, preferred_element_type=F32)
        xs_p = xs[:, pair * LANES:(pair + 1) * LANES]
        cc0, cc1 = cum_c[:, h0:h0 + 1], cum_c[:, h1:h1 + 1]
        dt_p = jnp.where(first_half, dt_c[:, h0:h0 + 1], dt_c[:, h1:h1 + 1])
        cc_p = jnp.where(first_half, cc0, cc1)
        last_p = jnp.where(first_half[0:1, :], cum_c[c - 1:c, h0:h0 + 1], cum_c[c - 1:c, h1:h1 + 1])
        xdt = xs_p * dt_p
        xdt_b = xdt.astype(BF16)
        yd = []
        for h, cc in ((h0, cc0), (h1, cc1)):
            seg = cc - cum_t[h:h + 1, :]
            decay = jnp.exp(jnp.where(causal, seg, -jnp.inf))
            yd.append(jnp.dot((cb * decay).astype(BF16), xdt_b, preferred_element_type=F32))
        y_diag = jnp.where(first_half, yd[0], yd[1])
        ent = st[pair * LANES:(pair + 1) * LANES, :]
        y_off = lax.dot_general(cm, ent.astype(BF16), _NT, preferred_element_type=F32) * jnp.exp(cc_p)
        xw = (xdt * jnp.exp(last_p - cc_p)).astype(BF16)
        contrib = lax.dot_general(xw, bm, _TN, preferred_element_type=F32)
        dec0 = jnp.exp(cum_c[c - 1:c, h0:h0 + 1])
        dec1 = jnp.exp(cum_c[c - 1:c, h1:h1 + 1])
        upper = lax.broadcasted_iota(jnp.int32, (LANES, SSD_STATE), 0) < SSD_HEAD_DIM
        st[pair * LANES:(pair + 1) * LANES, :] = ent * jnp.where(upper, dec0, dec1) + contrib
        ys.append(y_diag + y_off + dskip_ref[:, pair * LANES:(pair + 1) * LANES] * xs_p)

    y = jnp.concatenate(ys, axis=-1)
    u = y * _silu(z_ref[...])
    gw = SSD_WIDTH // SSD_GROUPS
    outs = []
    for g in range(SSD_GROUPS):
        ug = u[:, g * gw:(g + 1) * gw]
        outs.append(ug * lax.rsqrt(jnp.mean(ug * ug, axis=-1, keepdims=True) + EPS))
    o_ref[...] = jnp.concatenate(outs, axis=-1) * nw_ref[...]
    st_out_ref[...] = st[...]


def _rotate(x, cosf, sinf):
    w = x.shape[-1]
    half = RET_DK // 2
    ahead = pltpu.roll(x, w - half, 1)
    behind = pltpu.roll(x, half, 1)
    lane = lax.broadcasted_iota(jnp.int32, x.shape, 1) % RET_DK
    return x * cosf + jnp.where(lane < half, ahead, behind) * sinf


def _ret_prompt_kernel(rq_ref, rk_ref, rv_ref, rg_ref, cos_ref, sin_ref, dmat_ref, toend_ref, fstart_ref,
                       cdec_ref, nw_ref, o_ref, st_out_ref, st):
    c = CHUNK
    cosf, sinf = cos_ref[...], sin_ref[...]
    q = _rotate(rq_ref[...], cosf, sinf)
    k = _rotate(rk_ref[...], cosf, sinf) * (RET_DK ** -0.5)
    kw = (k * toend_ref[...]).astype(BF16)
    kb = k.astype(BF16)
    first_half = lax.broadcasted_iota(jnp.int32, (c, LANES), 1) < RET_DK
    ys = []
    for h in range(RET_HEADS):
        pair, odd = h // 2, h % 2
        sl = slice(pair * LANES, (pair + 1) * LANES)
        qm = jnp.where(first_half != bool(odd), q[:, sl], 0.0).astype(BF16)
        v = rv_ref[:, h * RET_DV:(h + 1) * RET_DV].astype(BF16)
        scores = lax.dot_general(qm, kb[:, sl], _NT, preferred_element_type=F32) * dmat_ref[h]
        y = jnp.dot(scores.astype(BF16), v, preferred_element_type=F32)
        ent = st[pair * LANES:(pair + 1) * LANES, :]
        y = y + jnp.dot(qm, ent.astype(BF16), preferred_element_type=F32) * fstart_ref[:, h * RET_DV:(h + 1) * RET_DV]
        contrib = lax.dot_general(kw[:, sl], v, _TN, preferred_element_type=F32)
        r0 = h * RET_DK
        st[r0:r0 + RET_DK, :] = (st[r0:r0 + RET_DK, :] * cdec_ref[r0:r0 + RET_DK, :]
                                 + contrib[odd * RET_DK:(odd + 1) * RET_DK, :])
        yc = y - jnp.mean(y, axis=-1, keepdims=True)
        ys.append(yc * lax.rsqrt(jnp.mean(yc * yc, axis=-1, keepdims=True) + EPS))
    y = jnp.concatenate(ys, axis=-1) * nw_ref[...]
    o_ref[...] = _silu(rg_ref[...]) * y
    st_out_ref[...] = st[...]


def _scan_kernel(*refs, n_ssd_in, n_ret_in):
    a = n_ssd_in
    b = a + n_ret_in
    xbuf, st_ssd, st_ret = refs[b + 4:]

    @pl.when(pl.program_id(0) == 0)
    def _():
        xbuf[0:8, :] = jnp.zeros((8, SSD_CONV_DIM), F32)
        st_ssd[...] = jnp.zeros_like(st_ssd)
        st_ret[...] = jnp.zeros_like(st_ret)

    _ssd_prompt_kernel(*refs[:a], *refs[b:b + 2], xbuf, st_ssd)
    _ret_prompt_kernel(*refs[a:b], *refs[b + 2:b + 4], st_ret)


def _scan_prompt(p, ssd_consts, cosf, sinf, ret_consts):
    n = p["z"].shape[0]
    c = CHUNK
    row = lambda w: pl.BlockSpec((c, w), lambda i: (i, 0))
    ssd_in = [p["z"], p["xbc"], p["dtc"], p["dtt"], *ssd_consts]
    ret_in = [p["rq"], p["rk"], p["rv"], p["rg"], cosf, sinf, *ret_consts]
    ssd_specs = ([row(SSD_WIDTH), row(SSD_CONV_DIM), row(LANES), pl.BlockSpec((16, c), lambda i: (0, i))]
                 + [_full(a.shape) for a in ssd_consts])
    ret_specs = ([row(RET_QK_WIDTH), row(RET_QK_WIDTH), row(RET_V_WIDTH), row(RET_V_WIDTH),
                  row(RET_QK_WIDTH), row(RET_QK_WIDTH)] + [_full(a.shape) for a in ret_consts])
    return pl.pallas_call(
        functools.partial(_scan_kernel, n_ssd_in=len(ssd_in), n_ret_in=len(ret_in)),
        grid=(n // c,),
        in_specs=ssd_specs + ret_specs,
        out_specs=[row(SSD_WIDTH), _full((SSD_WIDTH, SSD_STATE)), row(RET_V_WIDTH), _full((RET_QK_WIDTH, RET_DV))],
        out_shape=[jax.ShapeDtypeStruct((n, SSD_WIDTH), F32), jax.ShapeDtypeStruct((SSD_WIDTH, SSD_STATE), F32),
                   jax.ShapeDtypeStruct((n, RET_V_WIDTH), F32), jax.ShapeDtypeStruct((RET_QK_WIDTH, RET_DV), F32)],
        scratch_shapes=[pltpu.VMEM((c + 8, SSD_CONV_DIM), F32), pltpu.VMEM((SSD_WIDTH, SSD_STATE), F32),
                        pltpu.VMEM((RET_QK_WIDTH, RET_DV), F32)],
        compiler_params=_cparams("arbitrary"),
    )(*ssd_in, *ret_in)


def _neg_abs(x):
    sign = jnp.uint32(0x80000000)
    return lax.bitcast_convert_type(lax.bitcast_convert_type(x, jnp.uint32) | sign, F32)


def _sb_weights(z2, uneg, carries, key_off):
    tq, keys = z2.shape[0], z2.shape[1] // 2
    causal = None
    if key_off is not None:
        row = lax.broadcasted_iota(jnp.int32, (tq, keys), 0)
        col = lax.broadcasted_iota(jnp.int32, (tq, keys), 1)
        causal = col + key_off < row
    zs, sums = [], []
    for h in range(2):
        z = z2[:, h * keys:(h + 1) * keys]
        sp = jnp.log(1.0 + jnp.exp2(jnp.minimum(z, SB_MAX_LOGIT2))) * LOG2E
        if causal is not None:
            sp = jnp.where(causal, sp, 0.0)
        zs.append(z)
        sums.append(jnp.dot(sp.astype(BF16), uneg, preferred_element_type=F32))
    a_parts, new_carries = [], []
    for h in range(2):
        z, s, c = zs[h], sums[h], carries[h]
        a = jnp.exp2(z + s + jnp.concatenate([c] * (keys // LANES), axis=1))
        if causal is not None:
            a = jnp.where(causal, a, 0.0)
        a_parts.append(a.astype(BF16))
        new_carries.append(c + jnp.broadcast_to(s[:, 0:1], c.shape))
    return jnp.concatenate(a_parts, axis=1), new_carries


def _sb_prompt_kernel(dead_ref, q_ref, k_ref, v_ref, tail_ref, uneg_ref, o_ref, zbuf, abuf, cbuf, accbuf):
    pair, i = pl.program_id(0), pl.program_id(1)
    tq, keys = q_ref.shape[0], k_ref.shape[2]
    assert tq == 2 * keys
    qp, tail, uneg = q_ref[...], tail_ref[...], uneg_ref[...]
    zk = jnp.zeros((SB_HEAD_DIM, keys), BF16)

    def key_weights(blk):
        k2 = k_ref[blk]
        return jnp.concatenate([jnp.concatenate([k2[:SB_HEAD_DIM], zk], axis=1),
                                jnp.concatenate([zk, k2[SB_HEAD_DIM:]], axis=1), tail], axis=0)

    def value_weights(blk):
        v = v_ref[pl.ds(pl.multiple_of(blk * keys, keys), keys), :]
        lane_lo = lax.broadcasted_iota(jnp.int32, v.shape, 1) < SB_HEAD_DIM
        zero = jnp.zeros_like(v)
        return jnp.concatenate([jnp.where(lane_lo, v, zero), jnp.where(lane_lo, zero, v)], axis=0)

    def scores(blk):
        return jnp.dot(qp, key_weights(blk), preferred_element_type=F32)

    def weighted_values(a, vbd):
        return jnp.dot(a, vbd, preferred_element_type=F32)

    zero = jnp.zeros((tq, LANES), F32)
    hi, lo = 2 * i + 1, 2 * i
    zero_h = zero[keys:]
    z_hi = jnp.dot(qp[keys:], key_weights(hi), preferred_element_type=F32)
    a_hi, c_hi = _sb_weights(z_hi, uneg, [zero_h, zero_h], 0)
    accbuf[:keys, :] = zero_h
    accbuf[keys:, :] = weighted_values(a_hi, value_weights(hi))
    carries = [jnp.concatenate([zero_h, c], axis=0) for c in c_hi]
    a_lo, carries = _sb_weights(scores(lo), uneg, carries, 0)
    abuf[...] = a_lo
    cbuf[0], cbuf[1] = carries
    zbuf[...] = scores(jnp.maximum(lo - 1, 0))

    dead0, dead1 = dead_ref[2 * pair], dead_ref[2 * pair + 1]

    def two_blocks(t):
        hi = 2 * (i - t) - 1
        lo = hi - 1
        kb_lo, kb_next = key_weights(lo), key_weights(jnp.maximum(lo - 1, 0))
        vb_prev, vb_hi = value_weights(lo + 2), value_weights(hi)
        for r in range(tq // SB_ROWS):
            rows = slice(r * SB_ROWS, (r + 1) * SB_ROWS)
            acc = weighted_values(abuf[rows, :], vb_prev)
            a_hi, carries = _sb_weights(zbuf[rows, :], uneg, [cbuf[0, rows, :], cbuf[1, rows, :]], None)
            z_lo = jnp.dot(qp[rows], kb_lo, preferred_element_type=F32)
            acc = acc + weighted_values(a_hi, vb_hi)
            a_lo, carries = _sb_weights(z_lo, uneg, carries, None)
            abuf[rows, :] = a_lo
            cbuf[0, rows, :], cbuf[1, rows, :] = carries
            zbuf[rows, :] = jnp.dot(qp[rows], kb_next, preferred_element_type=F32)
            accbuf[rows, :] += acc

    odd = i % 2

    @pl.when(odd == 1)
    def _():
        two_blocks(0)

    def step(state):
        t, _ = state
        done = jnp.logical_and(jnp.max(cbuf[0]) < dead0, jnp.max(cbuf[1]) < dead1)
        two_blocks(t)
        two_blocks(t + 1)
        return t + 2, done.astype(jnp.int32)

    t_end, _ = lax.while_loop(lambda s: jnp.logical_and(s[0] < i, s[1] == 0), step, (odd, jnp.int32(0)))
    last = 2 * (i - t_end)
    o_ref[...] = accbuf[...] + weighted_values(abuf[...], value_weights(last))


def _sb_prompt(dead, qp, kblk, vb, tail, uneg, tq):
    n = qp.shape[0]
    npair = SB_HEADS // 2
    return pl.pallas_call(
        _sb_prompt_kernel,
        grid_spec=pltpu.PrefetchScalarGridSpec(
            num_scalar_prefetch=1,
            grid=(npair, n // tq),
            in_specs=[pl.BlockSpec((tq, 2 * LANES), lambda p, i, d: (i, p)),
                      pl.BlockSpec((None,) + kblk.shape[1:], lambda p, i, d: (p, 0, 0, 0)),
                      pl.BlockSpec((n, LANES), lambda p, i, d: (0, p)),
                      pl.BlockSpec(tail.shape, lambda p, i, d: (0, 0)),
                      pl.BlockSpec(uneg.shape, lambda p, i, d: (0, 0))],
            out_specs=pl.BlockSpec((tq, LANES), lambda p, i, d: (i, p)),
            scratch_shapes=[pltpu.VMEM((tq, 2 * SB_KEYS), F32), pltpu.VMEM((tq, 2 * SB_KEYS), BF16),
                            pltpu.VMEM((2, tq, LANES), F32), pltpu.VMEM((tq, LANES), F32)],
        ),
        out_shape=jax.ShapeDtypeStruct((n, SB_WIDTH), F32),
        compiler_params=_cparams("arbitrary", "arbitrary"),
    )(dead, qp, kblk, vb, tail, uneg)


def _merge_kernel(x_ref, a_ref, b_ref, c_ref, nw_ref, wg_ref, bg_ref, wa_ref, wb_ref, wc_ref, wo_ref, o_ref):
    x = x_ref[...]
    d = x.shape[-1]
    hb = _rms(x, nw_ref[...]).astype(BF16)
    merged = None
    for j, (br_ref, w_ref) in enumerate(((a_ref, wa_ref), (b_ref, wb_ref), (c_ref, wc_ref))):
        gate = jax.nn.sigmoid(jnp.dot(hb, wg_ref[:, j * d:(j + 1) * d], preferred_element_type=F32)
                              + bg_ref[:, j * d:(j + 1) * d])
        t = gate * jnp.dot(br_ref[...].astype(BF16), w_ref[...], preferred_element_type=F32)
        merged = t if merged is None else merged + t
    o_ref[...] = x + jnp.dot(merged.astype(BF16), wo_ref[...], preferred_element_type=F32)


def _merge(x, o_ssd, o_sb, o_ret, nw, wg, bg, wa, wb, wc, wo):
    n, d = x.shape
    tm = min(n, 256)
    row = lambda w: pl.BlockSpec((tm, w), lambda i: (i, 0))
    consts = [nw, wg, bg, wa, wb, wc, wo]
    return pl.pallas_call(
        _merge_kernel,
        grid=(n // tm,),
        in_specs=[row(d), row(o_ssd.shape[1]), row(o_sb.shape[1]), row(o_ret.shape[1])]
                 + [_full(a.shape) for a in consts],
        out_specs=row(d),
        out_shape=jax.ShapeDtypeStruct((n, d), F32),
        compiler_params=_cparams("arbitrary"),
    )(x, o_ssd, o_sb, o_ret, *consts)


def _mlp_kernel(x_ref, nw_ref, wu_ref, wd_ref, o_ref):
    x = x_ref[...]
    hb = _rms(x, nw_ref[...]).astype(BF16)
    up = jnp.maximum(jnp.dot(hb, wu_ref[...], preferred_element_type=F32), 0.0)
    o_ref[...] = x + jnp.dot((up * up).astype(BF16), wd_ref[...], preferred_element_type=F32)


def _mlp(x, nw, wu, wd):
    n, d = x.shape
    tm = min(n, 256)
    row = pl.BlockSpec((tm, d), lambda i: (i, 0))
    return pl.pallas_call(
        _mlp_kernel,
        grid=(n // tm,),
        in_specs=[row, _full(nw.shape), _full(wu.shape), _full(wd.shape)],
        out_specs=row,
        out_shape=jax.ShapeDtypeStruct((n, d), F32),
        compiler_params=_cparams("arbitrary"),
    )(x, nw, wu, wd)


def _dec_prep_kernel(xbc_ref, cp_ref, cw_ref, cb_ref, dtc_ref, dtb_ref, aneg_ref, rq_ref, rk_ref,
                     cos_ref, sin_ref, xc_ref, dt_ref, da_ref, q_ref, k_ref):
    acc = cb_ref[...] + cw_ref[SSD_CONV - 1:SSD_CONV, :] * xbc_ref[...]
    for k in range(SSD_CONV - 1):
        acc = acc + cw_ref[k:k + 1, :] * cp_ref[k]
    xc_ref[...] = _silu(acc)
    dt = _softplus(dtc_ref[...] + dtb_ref[...])
    dt_ref[...] = dt
    da_ref[...] = jnp.exp(dt * aneg_ref[...])
    q_ref[...] = _rotate(rq_ref[...], cos_ref[...], sin_ref[...])
    k_ref[...] = _rotate(rk_ref[...], cos_ref[...], sin_ref[...]) * (RET_DK ** -0.5)


def _dec_prep(xbc, cp, cw, cb, dtc, dtb, aneg, rq, rk, cosr, sinr):
    b = xbc.shape[0]
    args = [xbc, cp, cw, cb, dtc, dtb, aneg, rq, rk, cosr, sinr]
    return pl.pallas_call(
        _dec_prep_kernel,
        grid=(1,),
        in_specs=[_full(a.shape) for a in args],
        out_specs=[_full((b, SSD_CONV_DIM)), _full((b, LANES)), _full((b, LANES)),
                   _full((b, RET_QK_WIDTH)), _full((b, RET_QK_WIDTH))],
        out_shape=[jax.ShapeDtypeStruct((b, SSD_CONV_DIM), F32), jax.ShapeDtypeStruct((b, LANES), F32),
                   jax.ShapeDtypeStruct((b, LANES), F32), jax.ShapeDtypeStruct((b, RET_QK_WIDTH), F32),
                   jax.ShapeDtypeStruct((b, RET_QK_WIDTH), F32)],
        compiler_params=_cparams("arbitrary"),
    )(*args)


def _dec_state_kernel(s_ref, xs_ref, dt_ref, da_ref, bh_ref, ch_ref, r_ref, kc_ref, qc_ref, vr_ref, gam_ref,
                      s_out_ref, y_ref, r_out_ref, yr_ref):
    s_new = s_ref[...] * da_ref[...] + (xs_ref[...] * dt_ref[...]) * bh_ref[...]
    s_out_ref[...] = s_new
    y_ref[...] = jnp.sum(s_new * ch_ref[...], axis=-1, keepdims=True)
    r_new = r_ref[...] * gam_ref[...] + kc_ref[...] * vr_ref[...]
    r_out_ref[...] = r_new
    yr_ref[...] = jnp.sum(qc_ref[...] * r_new, axis=-2, keepdims=True)


def _dec_state(s, xs_col, dt_col, da_col, bh, ch, r, k_col, q_col, v_row, gam, nb):
    b = s.shape[0] // SSD_HEADS
    hs, hr = nb * SSD_HEADS, nb * RET_HEADS
    blk = lambda rows, a, c: pl.BlockSpec((rows, a, c), lambda i: (i, 0, 0))
    return pl.pallas_call(
        _dec_state_kernel,
        grid=(b // nb,),
        in_specs=[blk(hs, SSD_HEAD_DIM, SSD_STATE), blk(hs, SSD_HEAD_DIM, 1), blk(hs, 1, 1), blk(hs, 1, 1),
                  blk(hs, 1, SSD_STATE), blk(hs, 1, SSD_STATE),
                  blk(hr, RET_DK, RET_DV), blk(hr, RET_DK, 1), blk(hr, RET_DK, 1), blk(hr, 1, RET_DV),
                  blk(hr, 1, 1)],
        out_specs=[blk(hs, SSD_HEAD_DIM, SSD_STATE), blk(hs, SSD_HEAD_DIM, 1),
                   blk(hr, RET_DK, RET_DV), blk(hr, 1, RET_DV)],
        out_shape=[jax.ShapeDtypeStruct(s.shape, F32), jax.ShapeDtypeStruct(xs_col.shape, F32),
                   jax.ShapeDtypeStruct(r.shape, F32), jax.ShapeDtypeStruct(v_row.shape, F32)],
        compiler_params=_cparams("arbitrary"),
    )(s, xs_col, dt_col, da_col, bh, ch, r, k_col, q_col, v_row, gam)


def _dec_post_kernel(y_ref, xs_ref, dskip_ref, z_ref, snw_ref, yr_ref, rg_ref, rnw_ref, o_ssd_ref, o_ret_ref):
    u = (y_ref[...] + dskip_ref[...] * xs_ref[...]) * _silu(z_ref[...])
    gw = SSD_WIDTH // SSD_GROUPS
    outs = []
    for g in range(SSD_GROUPS):
        ug = u[:, g * gw:(g + 1) * gw]
        outs.append(ug * lax.rsqrt(jnp.mean(ug * ug, axis=-1, keepdims=True) + EPS))
    o_ssd_ref[...] = jnp.concatenate(outs, axis=-1) * snw_ref[...]
    ys = []
    for h in range(RET_HEADS):
        y = yr_ref[:, h * RET_DV:(h + 1) * RET_DV]
        yc = y - jnp.mean(y, axis=-1, keepdims=True)
        ys.append(yc * lax.rsqrt(jnp.mean(yc * yc, axis=-1, keepdims=True) + EPS))
    o_ret_ref[...] = _silu(rg_ref[...]) * (jnp.concatenate(ys, axis=-1) * rnw_ref[...])


def _dec_post(y, xs, dskip, z, snw, yr, rg, rnw):
    b = y.shape[0]
    args = [y, xs, dskip, z, snw, yr, rg, rnw]
    return pl.pallas_call(
        _dec_post_kernel,
        grid=(1,),
        in_specs=[_full(a.shape) for a in args],
        out_specs=[_full((b, SSD_WIDTH)), _full((b, RET_V_WIDTH))],
        out_shape=[jax.ShapeDtypeStruct((b, SSD_WIDTH), F32), jax.ShapeDtypeStruct((b, RET_V_WIDTH), F32)],
        compiler_params=_cparams("arbitrary"),
    )(*args)


HEAD_ROWS = 16


def _sb_decode_kernel(pt_ref, bias_ref, qbdt_ref, knewt_ref, vnewt_ref, uo_ref, diag_ref, *rest, pages, past):
    k_refs, v_refs = rest[:pages], rest[pages:2 * pages]
    o_ref, acc, carry = rest[2 * pages], rest[2 * pages + 1], rest[2 * pages + 2]
    bi, c = pl.program_id(0), pl.program_id(1)
    bias = bias_ref[...]
    qbdt = qbdt_ref[0]
    uneg = uo_ref[...]

    def sums(kt, visible):
        z = jnp.dot(qbdt, kt.astype(BF16), preferred_element_type=F32) + bias
        sp = jnp.maximum(z, 0.0) + jnp.log(1.0 + jnp.exp(_neg_abs(z)))
        if visible is not None:
            sp = jnp.where(visible, sp, 0.0)
        return z, _dot_rhs01(sp, uneg, 2)

    def weigh(z, s, carry_in, vt, visible):
        a = jnp.exp(z + s + carry_in)
        if visible is not None:
            a = jnp.where(visible, a, 0.0)
        out = lax.dot_general(a.astype(BF16), vt.astype(BF16), _NT, preferred_element_type=F32)
        return out, carry_in + jnp.broadcast_to(s[:, 0:1], carry_in.shape)

    @pl.when(c == 0)
    def _():
        lane = lax.broadcasted_iota(jnp.int32, (HEAD_ROWS, LANES), 1)
        key_pos = jnp.full((HEAD_ROWS, LANES), past, jnp.int32)
        visible = (lane == bi) & (key_pos < past)
        z, s = sums(knewt_ref[...], visible)
        out, cnew = weigh(z, s, jnp.zeros((HEAD_ROWS, LANES), F32), vnewt_ref[...], visible)
        acc[...] = out
        carry[...] = cnew

    zs = [sums(k_ref[...], None) for k_ref in k_refs]
    cur = carry[...]
    total = acc[...]
    for (z, s), v_ref in zip(zs, v_refs):
        out, cur = weigh(z, s, cur, v_ref[...], None)
        total = total + out
    acc[...] = total
    carry[...] = cur

    @pl.when(c == pl.num_programs(1) - 1)
    def _():
        o_ref[0] = jnp.sum(acc[...] * diag_ref[...], axis=0, keepdims=True)


def _sb_decode(page_table, cache_kt, cache_vt, layer, bias_col, qbdt, knewt, vnewt, uo, diag, pages):
    b = qbdt.shape[0]
    n_pages = page_table.shape[0] // b
    n_phys = cache_kt.shape[0] // 2
    nc = n_pages // pages
    base = layer * n_phys

    def page_spec(j):
        def imap(bi, c, pt):
            logical = (nc - 1 - c) * pages + (pages - 1 - j)
            return (base + pt[bi * n_pages + logical], 0, 0)
        return pl.BlockSpec((None, SB_WIDTH, PAGE_SIZE), imap)

    const = lambda a: pl.BlockSpec(a.shape, lambda bi, c, pt: (0,) * a.ndim)
    return pl.pallas_call(
        functools.partial(_sb_decode_kernel, pages=pages, past=n_pages * PAGE_SIZE),
        grid_spec=pltpu.PrefetchScalarGridSpec(
            num_scalar_prefetch=1,
            grid=(b, nc),
            in_specs=[const(bias_col), pl.BlockSpec((1,) + qbdt.shape[1:], lambda bi, c, pt: (bi, 0, 0)),
                      const(knewt), const(vnewt), const(uo), const(diag)]
                     + [page_spec(j) for j in range(pages)] * 2,
            out_specs=pl.BlockSpec((1, 1, SB_WIDTH), lambda bi, c, pt: (bi, 0, 0)),
            scratch_shapes=[pltpu.VMEM((HEAD_ROWS, SB_WIDTH), F32), pltpu.VMEM((HEAD_ROWS, LANES), F32)],
        ),
        out_shape=jax.ShapeDtypeStruct((b, 1, SB_WIDTH), F32),
        compiler_params=_cparams("arbitrary", "arbitrary"),
    )(page_table, bias_col, qbdt, knewt, vnewt, uo, diag, *([cache_kt] * pages), *([cache_vt] * pages))


def _rope_tables(pos):
    half = RET_DK // 2
    inv = ROPE_BASE ** (-jnp.arange(half, dtype=F32) / half)
    ang = pos.astype(F32)[:, None] * inv[None, :]
    cos, sin = jnp.cos(ang), jnp.sin(ang)
    cosf = jnp.tile(jnp.concatenate([cos, cos], axis=-1), (1, RET_HEADS))
    sinf = jnp.tile(jnp.concatenate([-sin, sin], axis=-1), (1, RET_HEADS))
    return cosf, sinf


def _ret_tables():
    log_gamma = jnp.log1p(-jnp.exp2(-5.0 - jnp.arange(RET_HEADS, dtype=F32)))
    idx = jnp.arange(CHUNK, dtype=F32)
    rel = idx[:, None] - idx[None, :]
    dmat = jnp.exp(jnp.where((rel >= 0)[None], rel[None] * log_gamma[:, None, None], -jnp.inf))
    to_end = jnp.exp((CHUNK - 1 - idx)[None, :] * log_gamma[:, None])
    from_start = jnp.exp((idx + 1.0)[None, :] * log_gamma[:, None])
    toend = jnp.repeat(to_end.T, RET_DK, axis=1)
    fstart = jnp.repeat(from_start.T, RET_DV, axis=1)
    cdec = jnp.broadcast_to(jnp.repeat(jnp.exp(CHUNK * log_gamma), RET_DK)[:, None], (RET_QK_WIDTH, RET_DV))
    return log_gamma, dmat, toend, fstart, cdec


def _np01(a):
    return jnp.asarray(np.asarray(a, np.float32), BF16)


def kernel(x_prompt, x_sample, cache_sb_k, cache_sb_v, page_table, state_ssd_conv, state_ssd, state_ret,
           norm_mix, w_in, conv_w, conv_b, dt_bias, a_log, d_skip, ssd_norm_w, sb_q_norm, sb_k_norm,
           sb_bias, ret_norm_w, w_gate, b_gate, w_br_ssd, w_br_sb, w_br_ret, w_out, norm_mlp, w_up, w_down):
    b_p, seq, d_model = x_prompt.shape
    b_s, t_new, _ = x_sample.shape
    depth = w_in.shape[0]
    n_pages = page_table.shape[1]
    past = n_pages * PAGE_SIZE
    assert b_p == 1 and t_new == 1, "one prompt sequence and one new token per sample sequence"
    assert seq % SB_TQ == 0 and b_s % 8 == 0 and b_s <= LANES and n_pages % DEC_PAGES == 0

    ii = np.arange(CHUNK)
    tril = _np01(ii[:, None] >= ii[None, :])
    triu = _np01(ii[:, None] <= ii[None, :])
    neg_suffix = lambda m: _np01(-(np.arange(m)[:, None] >= np.arange(m)[None, :]).astype(np.float32))
    uneg_page, uneg_blk = neg_suffix(PAGE_SIZE), neg_suffix(SB_KEYS)
    tail_np = np.zeros((LANES, 2 * SB_KEYS), np.float32)
    tail_np[0:2, :SB_KEYS] = 1.0
    tail_np[2:4, SB_KEYS:] = 1.0
    tail = _np01(tail_np)
    hh = np.arange(SB_WIDTH) // SB_HEAD_DIM
    seg = _np01(hh[:, None] == hh[None, :])
    head_of = np.arange(HEAD_ROWS)[:, None] == hh[None, :]
    diag = jnp.asarray(head_of, F32)
    log_gamma, dmat, toend, fstart, cdec = _ret_tables()
    cos_p, sin_p = _rope_tables(jnp.arange(seq, dtype=jnp.int32))
    cos_s, sin_s = _rope_tables(past + jnp.arange(t_new, dtype=jnp.int32))
    gam = jnp.tile(jnp.exp(log_gamma), b_s).reshape(b_s * RET_HEADS, 1, 1)

    sizes = [SSD_WIDTH, SSD_CONV_DIM, SSD_HEADS, SB_WIDTH, SB_WIDTH, SB_WIDTH,
             RET_QK_WIDTH, RET_QK_WIDTH, RET_V_WIDTH, RET_V_WIDTH]
    offs = np.concatenate([[0], np.cumsum(sizes)])
    cols = {name: slice(int(offs[j]), int(offs[j + 1]))
            for j, name in enumerate(["z", "xbc", "dt", "sq", "sk", "sv", "rq", "rk", "rv", "rg"])}
    pt_flat = page_table.reshape(-1)
    page_view = lambda c: jnp.transpose(c, (0, 1, 3, 4, 2)).reshape(depth * c.shape[1], SB_WIDTH, PAGE_SIZE)
    ckt, cvt = page_view(cache_sb_k), page_view(cache_sb_v)

    xp = x_prompt.reshape(seq, d_model)
    xs = x_sample.reshape(b_s, d_model)
    outs = [[] for _ in range(10)]
    for l in range(depth):
        wl = w_in[l]
        w_main = jnp.concatenate([wl[:, cols[c]] for c in ("z", "xbc", "sq", "sv", "rq", "rk", "rv", "rg")],
                                 axis=1).astype(BF16)
        w_kvt = jnp.concatenate([wl[:, cols["sk"]], wl[:, cols["sv"]]], axis=1).T.astype(BF16)
        w_dtc = jnp.pad(wl[:, cols["dt"]], ((0, 0), (0, LANES - SSD_HEADS))).astype(BF16)
        w_dtt = w_dtc[:, :HEAD_ROWS].T
        nw = norm_mix[l][None, :]
        qn = jnp.tile(sb_q_norm[l], SB_HEADS)[None, :]
        kn_col = jnp.tile(sb_k_norm[l], SB_HEADS)[:, None]
        a_neg = -jnp.exp(a_log[l])
        pad_h = lambda v: jnp.pad(v, (0, LANES - SSD_HEADS))
        dtb_row, aneg_row = pad_h(dt_bias[l])[None, :], pad_h(a_neg)[None, :]
        dtb_col, aneg_col = pad_h(dt_bias[l])[:HEAD_ROWS, None], pad_h(a_neg)[:HEAD_ROWS, None]
        dskip = jnp.repeat(d_skip[l], SSD_HEAD_DIM)[None, :]
        snw, rnw = ssd_norm_w[l][None, :], ret_norm_w[l][None, :]
        cw, cb = conv_w[l], conv_b[l][None, :]
        wg, bg = w_gate[l].astype(BF16), b_gate[l][None, :]
        wa, wb, wc = w_br_ssd[l].astype(BF16), w_br_sb[l].astype(BF16), w_br_ret[l].astype(BF16)
        wo, wu, wd = w_out[l].astype(BF16), w_up[l].astype(BF16), w_down[l].astype(BF16)
        nm = norm_mlp[l][None, :]
        kv_out = lambda t, b, rows: jnp.transpose(t.reshape(SB_HEADS, SB_HEAD_DIM, b, rows), (2, 3, 0, 1))

        bias2 = sb_bias[l] * LOG2E
        b_hi = bias2.astype(BF16)
        b_lo = (bias2 - b_hi.astype(F32)).astype(BF16)
        qbias = jnp.pad(jnp.stack([b_hi, b_lo], axis=1).reshape(SB_HEADS // 2, 4),
                        ((0, 0), (0, LANES - 4))).reshape(1, SB_WIDTH)
        p = _in_proj(xp, nw, w_main, w_kvt, w_dtc, w_dtt, qn, kn_col, seg, qbias)
        o_ssd, st_ssd, o_ret, st_ret = _scan_prompt(
            p, [cw, cb, dtb_row, dtb_col, aneg_row, aneg_col, dskip, snw, tril, triu],
            cos_p, sin_p, [dmat, toend, fstart, cdec, rnw])
        z_max = (SB_HEAD_DIM ** 0.5 * LOG2E * 1.02 * jnp.max(jnp.abs(sb_q_norm[l])) * jnp.max(jnp.abs(sb_k_norm[l]))
                 + bias2 + 0.1)
        dead = -151.0 - z_max
        o_sb = _sb_prompt(dead, p["qp"], p["kblk"], p["vb"], tail, uneg_blk, SB_TQ)
        xp = _mlp(_merge(xp, o_ssd, o_sb, o_ret, nw, wg, bg, wa, wb, wc, wo), nm, wu, wd)
        outs[0].append(kv_out(p["kt"], b_p, seq))
        outs[1].append(kv_out(p["vt"], b_p, seq))
        outs[4].append(p["xbc"][seq - (SSD_CONV - 1):].reshape(b_p, SSD_CONV - 1, SSD_CONV_DIM))
        outs[6].append(st_ssd.reshape(b_p, SSD_HEADS, SSD_HEAD_DIM, SSD_STATE))
        outs[8].append(st_ret.reshape(b_p, RET_HEADS, RET_DK, RET_DV))

        s = _in_proj(xs, nw, w_main, w_kvt, w_dtc, w_dtt, qn, kn_col, seg)
        cp = jnp.swapaxes(state_ssd_conv[l], 0, 1)
        xc, dt, da, rq, rk = _dec_prep(s["xbc"], cp, cw, cb, s["dtc"], dtb_row, aneg_row, s["rq"], s["rk"],
                                       cos_s, sin_s)
        x_ssd = xc[:, :SSD_WIDTH]
        per_head = lambda m: jnp.repeat(m.reshape(b_s, SSD_GROUPS, 1, SSD_STATE), SSD_HEADS // SSD_GROUPS,
                                        axis=1).reshape(b_s * SSD_HEADS, 1, SSD_STATE)
        bh = per_head(xc[:, SSD_WIDTH:SSD_WIDTH + SSD_GROUPS * SSD_STATE])
        ch = per_head(xc[:, SSD_WIDTH + SSD_GROUPS * SSD_STATE:])
        s_new, y_col, r_new, yr = _dec_state(
            state_ssd[l].reshape(b_s * SSD_HEADS, SSD_HEAD_DIM, SSD_STATE),
            x_ssd.reshape(b_s * SSD_HEADS, SSD_HEAD_DIM, 1),
            dt[:, :SSD_HEADS].reshape(b_s * SSD_HEADS, 1, 1), da[:, :SSD_HEADS].reshape(b_s * SSD_HEADS, 1, 1),
            bh, ch,
            state_ret[l].reshape(b_s * RET_HEADS, RET_DK, RET_DV),
            rk.reshape(b_s * RET_HEADS, RET_DK, 1), rq.reshape(b_s * RET_HEADS, RET_DK, 1),
            s["rv"].reshape(b_s * RET_HEADS, 1, RET_DV), gam, 8)
        o_ssd_s, o_ret_s = _dec_post(y_col.reshape(b_s, SSD_WIDTH), x_ssd, dskip, s["z"], snw,
                                     yr.reshape(b_s, RET_V_WIDTH), s["rg"], rnw)
        qbdt = jnp.where(head_of[None], s["qb"][:, None, :], jnp.zeros((), BF16))
        bias_col = pad_h(sb_bias[l])[:HEAD_ROWS, None]
        pad_b = lambda t: jnp.pad(t, ((0, 0), (0, LANES - b_s)))
        o_sb_s = _sb_decode(pt_flat, ckt, cvt, l, bias_col, qbdt, pad_b(s["kt"]), pad_b(s["vt"]), uneg_page, diag,
                            DEC_PAGES).reshape(b_s, SB_WIDTH)
        xs = _mlp(_merge(xs, o_ssd_s, o_sb_s, o_ret_s, nw, wg, bg, wa, wb, wc, wo), nm, wu, wd)
        outs[2].append(kv_out(s["kt"], b_s, t_new))
        outs[3].append(kv_out(s["vt"], b_s, t_new))
        outs[5].append(jnp.concatenate([state_ssd_conv[l][:, 1:], s["xbc"][:, None, :]], axis=1))
        outs[7].append(s_new.reshape(b_s, SSD_HEADS, SSD_HEAD_DIM, SSD_STATE))
        outs[9].append(r_new.reshape(b_s, RET_HEADS, RET_DK, RET_DV))

    stacked = [jnp.stack(o) for o in outs]
    return (xp.reshape(b_p, seq, d_model), xs.reshape(b_s, t_new, d_model), *stacked)
```
